```python
import jax, jax.numpy as jnp
from jax import lax
import numpy as np

D_MODEL = 2048
BATCH = 1
SEQ = 16384
DEPTH = 2

HEAD_DIM = 128
ATTN_WIDTH = D_MODEL // 2
N_ATTN_HEADS = ATTN_WIDTH // HEAD_DIM
CONV_WIDTH = D_MODEL - ATTN_WIDTH
MIX_WIDTH = ATTN_WIDTH + CONV_WIDTH
IN_PROJ_WIDTH = 3 * ATTN_WIDTH + 2 * CONV_WIDTH
CONV_KERNEL = 31
MOBA_BLOCK = 256
MOBA_TOP_K = 3
Q_CHUNK = 128
ROPE_THETA = 10000.0
D_FF = 5632
N_EXPERTS = 8
TOP_K_EXPERTS = 2
N_DENSE = (DEPTH + 1) // 2
N_MOE = DEPTH // 2
LN_EPS = 1e-5
DEEPNORM_ALPHA = (2.0 * DEPTH) ** 0.25
DEEPNORM_BETA = (8.0 * DEPTH) ** -0.25

kernel_name = "hymba_conformer_moba_moe_deepnorm"


def layer_norm(x, g, b):
    xf = x.astype(jnp.float32)
    mu = jnp.mean(xf, axis=-1, keepdims=True)
    var = jnp.mean(jnp.square(xf - mu), axis=-1, keepdims=True)
    return ((xf - mu) * lax.rsqrt(var + LN_EPS) * g + b).astype(x.dtype)


def rotary(x, pos):
    half = HEAD_DIM // 2
    inv_freq = jnp.power(ROPE_THETA, -jnp.arange(half, dtype=jnp.float32) / half)
    ang = pos.astype(jnp.float32)[:, None] * inv_freq[None, :]
    cos = jnp.cos(ang)[None, :, None, :]
    sin = jnp.sin(ang)[None, :, None, :]
    x1 = x[..., :half].astype(jnp.float32)
    x2 = x[..., half:].astype(jnp.float32)
    return jnp.concatenate([x1 * cos - x2 * sin, x2 * cos + x1 * sin], axis=-1).astype(x.dtype)


def moba_attention(q, k, v):
    B, S, H, D = q.shape
    nb = -(-S // MOBA_BLOCK)
    pad = nb * MOBA_BLOCK - S
    if pad:
        k = jnp.pad(k, ((0, 0), (0, pad), (0, 0), (0, 0)))
        v = jnp.pad(v, ((0, 0), (0, pad), (0, 0), (0, 0)))
    kk = min(MOBA_TOP_K, nb)
    scale = D ** -0.5
    kb = k.reshape(B, nb, MOBA_BLOCK, H, D).transpose(0, 3, 1, 2, 4)
    vb = v.reshape(B, nb, MOBA_BLOCK, H, D).transpose(0, 3, 1, 2, 4)
    k_mean = jnp.mean(kb.astype(jnp.float32), axis=3)
    n_chunks = S // Q_CHUNK
    qc = q.reshape(B, n_chunks, Q_CHUNK, H, D).transpose(1, 0, 3, 2, 4)
    b_idx = jnp.arange(B)[:, None, None, None]
    h_idx = jnp.arange(H)[None, :, None, None]
    blk_ids = jnp.arange(nb)

    def one_chunk(args):
        c, q_c = args
        q_start = c * Q_CHUNK
        own = q_start // MOBA_BLOCK
        q_pos = q_start + jnp.arange(Q_CHUNK)
        gate = jnp.einsum('bhqd,bhnd->bhqn', q_c.astype(jnp.float32), k_mean)
        gate = jnp.where(blk_ids[None, None, None, :] < own, gate, -jnp.inf)
        _, sel = lax.top_k(gate, kk)
        valid = sel < own
        k_sel = kb[b_idx, h_idx, sel]
        v_sel = vb[b_idx, h_idx, sel]
        s_sel = jnp.einsum('bhqd,bhqkld->bhqkl', q_c, k_sel).astype(jnp.float32) * scale
        s_sel = jnp.where(valid[..., None], s_sel, -jnp.inf)
        k_own = lax.dynamic_index_in_dim(kb, own, axis=2, keepdims=False)
        v_own = lax.dynamic_index_in_dim(vb, own, axis=2, keepdims=False)
        s_own = jnp.einsum('bhqd,bhld->bhql', q_c, k_own).astype(jnp.float32) * scale
        k_pos = own * MOBA_BLOCK + jnp.arange(MOBA_BLOCK)
        s_own = jnp.where(k_pos[None, :] <= q_pos[:, None], s_own, -jnp.inf)
        scores = jnp.concatenate(
            [s_sel.reshape(B, H, Q_CHUNK, kk * MOBA_BLOCK), s_own], axis=-1)
        p = jax.nn.softmax(scores, axis=-1)
        p_sel = p[..., :kk * MOBA_BLOCK].reshape(B, H, Q_CHUNK, kk, MOBA_BLOCK).astype(q.dtype)
        p_own = p[..., kk * MOBA_BLOCK:].astype(q.dtype)
        return (jnp.einsum('bhqkl,bhqkld->bhqd', p_sel, v_sel)
                + jnp.einsum('bhql,bhld->bhqd', p_own, v_own))

    out = lax.map(one_chunk, (jnp.arange(n_chunks), qc))
    return out.transpose(1, 0, 3, 2, 4).reshape(B, S, H * D)


def causal_depthwise_conv(u, w, b):
    C = u.shape[-1]
    y = lax.conv_general_dilated(
        u, w[:, None, :], window_strides=(1,), padding=[(CONV_KERNEL - 1, 0)],
        dimension_numbers=('NWC', 'WIO', 'NWC'), feature_group_count=C)
    return y + b


def hybrid_mixer(x, w_in, conv_dw_w, conv_dw_b, conv_ln_g, conv_ln_b, w_out, pos):
    B, S, _ = x.shape
    proj = x @ w_in
    q, k, v, glu_a, glu_g = jnp.split(
        proj, [ATTN_WIDTH, 2 * ATTN_WIDTH, 3 * ATTN_WIDTH, 3 * ATTN_WIDTH + CONV_WIDTH], axis=-1)
    q = rotary(q.reshape(B, S, N_ATTN_HEADS, HEAD_DIM), pos)
    k = rotary(k.reshape(B, S, N_ATTN_HEADS, HEAD_DIM), pos)
    v = v.reshape(B, S, N_ATTN_HEADS, HEAD_DIM)
    attn = moba_attention(q, k, v)
    u = glu_a * jax.nn.sigmoid(glu_g)
    u = causal_depthwise_conv(u, conv_dw_w, conv_dw_b)
    u = jax.nn.silu(layer_norm(u, conv_ln_g, conv_ln_b))
    return jnp.concatenate([attn, u], axis=-1) @ w_out


def swiglu(x, w_gate, w_up, w_down):
    return (jax.nn.silu(x @ w_gate) * (x @ w_up)) @ w_down


def moe_swiglu(x, router_w, e_gate, e_up, e_down):
    logits = (x @ router_w).astype(jnp.float32)
    top_vals, top_idx = lax.top_k(logits, TOP_K_EXPERTS)
    gates = jax.nn.softmax(top_vals, axis=-1)
    combine = jnp.sum(jax.nn.one_hot(top_idx, N_EXPERTS, dtype=jnp.float32)
                      * gates[..., None], axis=-2).astype(x.dtype)
    y = jnp.zeros_like(x)
    for e in range(N_EXPERTS):
        y = y + combine[..., e:e + 1] * swiglu(x, e_gate[e], e_up[e], e_down[e])
    return y


def setup_inputs(seed: int = 0) -> dict:
    key = jax.random.key(seed)
    ks = jax.random.split(key, 20)
    f32 = jnp.float32
    nrm = lambda k, shape, s: jax.random.normal(k, shape, f32) * s
    d_scale = D_MODEL ** -0.5
    col_scale = jnp.concatenate([
        jnp.ones((2 * ATTN_WIDTH,), f32),
        jnp.full((ATTN_WIDTH,), DEEPNORM_BETA, f32),
        jnp.ones((2 * CONV_WIDTH,), f32)])
    w_in = nrm(ks[1], (DEPTH, D_MODEL, IN_PROJ_WIDTH), d_scale) * col_scale
    return {
        "x": jax.random.normal(ks[0], (BATCH, SEQ, D_MODEL), f32),
        "w_in": w_in,
        "conv_dw_w": nrm(ks[2], (DEPTH, CONV_KERNEL, CONV_WIDTH), CONV_KERNEL ** -0.5),
        "conv_dw_b": nrm(ks[3], (DEPTH, CONV_WIDTH), 0.02),
        "conv_ln_g": 1.0 + nrm(ks[4], (DEPTH, CONV_WIDTH), 0.02),
        "conv_ln_b": nrm(ks[5], (DEPTH, CONV_WIDTH), 0.02),
        "w_out": nrm(ks[6], (DEPTH, MIX_WIDTH, D_MODEL), MIX_WIDTH ** -0.5 * DEEPNORM_BETA),
        "ln_mix_g": 1.0 + nrm(ks[7], (DEPTH, D_MODEL), 0.02),
        "ln_mix_b": nrm(ks[8], (DEPTH, D_MODEL), 0.02),
        "ln_ffn_g": 1.0 + nrm(ks[9], (DEPTH, D_MODEL), 0.02),
        "ln_ffn_b": nrm(ks[10], (DEPTH, D_MODEL), 0.02),
        "ffn_w_gate": nrm(ks[11], (N_DENSE, D_MODEL, D_FF), d_scale),
        "ffn_w_up": nrm(ks[12], (N_DENSE, D_MODEL, D_FF), d_scale),
        "ffn_w_down": nrm(ks[13], (N_DENSE, D_FF, D_MODEL), D_FF ** -0.5 * DEEPNORM_BETA),
        "router_w": nrm(ks[14], (N_MOE, D_MODEL, N_EXPERTS), d_scale),
        "expert_w_gate": nrm(ks[15], (N_MOE, N_EXPERTS, D_MODEL, D_FF), d_scale),
        "expert_w_up": nrm(ks[16], (N_MOE, N_EXPERTS, D_MODEL, D_FF), d_scale),
        "expert_w_down": nrm(ks[17], (N_MOE, N_EXPERTS, D_FF, D_MODEL), D_FF ** -0.5 * DEEPNORM_BETA),
    }


def reference(x, w_in, conv_dw_w, conv_dw_b, conv_ln_g, conv_ln_b, w_out,
              ln_mix_g, ln_mix_b, ln_ffn_g, ln_ffn_b,
              ffn_w_gate, ffn_w_up, ffn_w_down,
              router_w, expert_w_gate, expert_w_up, expert_w_down):
    S = x.shape[1]
    pos = jnp.arange(S, dtype=jnp.int32)
    h = x
    for l in range(DEPTH):
        mix = hybrid_mixer(h, w_in[l], conv_dw_w[l], conv_dw_b[l], conv_ln_g[l], conv_ln_b[l],
                           w_out[l], pos)
        h = layer_norm(DEEPNORM_ALPHA * h + mix, ln_mix_g[l], ln_mix_b[l])
        if l % 2 == 0:
            f = swiglu(h, ffn_w_gate[l // 2], ffn_w_up[l // 2], ffn_w_down[l // 2])
        else:
            f = moe_swiglu(h, router_w[l // 2], expert_w_gate[l // 2],
                           expert_w_up[l // 2], expert_w_down[l // 2])
        h = layer_norm(DEEPNORM_ALPHA * h + f, ln_ffn_g[l], ln_ffn_b[l])
    return h
```

```python
import functools

import jax
import jax.numpy as jnp
from jax import lax
from jax.experimental import pallas as pl
from jax.experimental.pallas import tpu as pltpu

F32 = jnp.float32
BF16 = jnp.bfloat16

HEAD_DIM = 128
MOBA_BLOCK = 256
MOBA_TOP_K = 3
CONV_KERNEL = 31
ROPE_THETA = 10000.0
LN_EPS = 1e-5
TOP_K_EXPERTS = 2

LANES = 128
SUBLANES = 8
CONV_HALO = 32
CONV_ROWS = 32
MASK_PENALTY = -1e30
VMEM_LIMIT = 56 * 1024 * 1024

_DN_T = (((1,), (1,)), ((), ()))


def _params(n_axes):
    return pltpu.CompilerParams(dimension_semantics=("arbitrary",) * n_axes,
                                vmem_limit_bytes=VMEM_LIMIT)


def _sigmoid(x):
    return 1.0 / (1.0 + jnp.exp(-x))


def _layer_norm(z, g, b):
    mu = jnp.mean(z, axis=-1, keepdims=True)
    zc = z - mu
    var = jnp.mean(zc * zc, axis=-1, keepdims=True)
    return zc * lax.rsqrt(var + LN_EPS) * g + b


def _split_bf16(x):
    hi = x.astype(BF16)
    lo = (x - hi.astype(F32)).astype(BF16)
    return hi, lo


def _qkv_kernel(x_ref, w_ref, cos_ref, sin_ref, o_ref, *, n_rot_blocks):
    j = pl.program_id(1)
    acc = jnp.dot(x_ref[...], w_ref[...], preferred_element_type=F32)

    @pl.when(j < n_rot_blocks)
    def _():
        cos = cos_ref[0]
        sin = sin_ref[0]
        for c in range(0, acc.shape[1], HEAD_DIM):
            a = acc[:, c:c + HEAD_DIM]
            r = a * cos + pltpu.roll(a, HEAD_DIM // 2, 1) * sin
            o_ref[:, c:c + HEAD_DIM] = r.astype(o_ref.dtype)

    @pl.when(j >= n_rot_blocks)
    def _():
        o_ref[...] = acc.astype(o_ref.dtype)


def _qkv_proj(xb, w, cos_tab, sin_tab, attn_w, tm, tn):
    S, D = xb.shape
    n_sec = attn_w // tn
    return pl.pallas_call(
        functools.partial(_qkv_kernel, n_rot_blocks=2 * n_sec),
        grid=(S // tm, 3 * n_sec),
        in_specs=[
            pl.BlockSpec((tm, D), lambda i, j: (i, 0)),
            pl.BlockSpec((D, tn), lambda i, j: (0, j)),
            pl.BlockSpec((1, tm, HEAD_DIM), lambda i, j: (jnp.minimum(j // n_sec, 1), i, 0)),
            pl.BlockSpec((1, tm, HEAD_DIM), lambda i, j: (jnp.minimum(j // n_sec, 1), i, 0)),
        ],
        out_specs=pl.BlockSpec((tm, tn), lambda i, j: (i, j)),
        out_shape=jax.ShapeDtypeStruct((S, 3 * attn_w), BF16),
        compiler_params=_params(2),
        name="qkv_proj",
    )(xb, w, cos_tab, sin_tab)


def _glu_kernel(x_ref, wa_ref, wg_ref, o_ref):
    x = x_ref[...]
    a = jnp.dot(x, wa_ref[...], preferred_element_type=F32)
    g = jnp.dot(x, wg_ref[...], preferred_element_type=F32)
    o_ref[...] = a * _sigmoid(g)


def _glu_proj(xb, w, attn_w, conv_w, tm, tn):
    S, D = xb.shape
    a0 = 3 * attn_w // tn
    g0 = (3 * attn_w + conv_w) // tn
    return pl.pallas_call(
        _glu_kernel,
        grid=(S // tm, conv_w // tn),
        in_specs=[
            pl.BlockSpec((tm, D), lambda i, j: (i, 0)),
            pl.BlockSpec((D, tn), lambda i, j: (0, a0 + j)),
            pl.BlockSpec((D, tn), lambda i, j: (0, g0 + j)),
        ],
        out_specs=pl.BlockSpec((tm, tn), lambda i, j: (i, j)),
        out_shape=jax.ShapeDtypeStruct((S, conv_w), F32),
        compiler_params=_params(2),
        name="glu_proj",
    )(xb, w, w)


def _conv_kernel(u_ref, halo_ref, w_ref, b_ref, g_ref, beta_ref, o_ref, ext_ref, y_ref):
    i = pl.program_id(0)
    tc = u_ref.shape[0]
    ext_ref[0:CONV_HALO, :] = jnp.where(i > 0, halo_ref[...], 0.0)
    ext_ref[CONV_HALO:CONV_HALO + tc, :] = u_ref[...]
    bias = b_ref[...]
    g = g_ref[...]
    beta = beta_ref[...]
    first = CONV_HALO - (CONV_KERNEL - 1)

    def body(r, carry):
        base = pl.multiple_of(r * CONV_ROWS, CONV_ROWS)
        for c in range(0, u_ref.shape[1], LANES):
            win = ext_ref[pl.ds(base, 2 * CONV_ROWS), c:c + LANES]
            acc = jnp.zeros((CONV_ROWS, LANES), F32) + bias[:, c:c + LANES]
            for shift in range(SUBLANES):
                taps = [k for k in range(CONV_KERNEL) if (first + k) % SUBLANES == shift]
                shifted = win[shift:shift + 2 * CONV_ROWS - (SUBLANES if shift else 0)]
                for k in taps:
                    a = (first + k) // SUBLANES * SUBLANES
                    acc = acc + w_ref[k:k + 1, c:c + LANES] * shifted[a:a + CONV_ROWS]
            y_ref[pl.ds(base, CONV_ROWS), c:c + LANES] = acc
        y = _layer_norm(y_ref[pl.ds(base, CONV_ROWS), :], g, beta)
        o_ref[pl.ds(base, CONV_ROWS), :] = (y * _sigmoid(y)).astype(o_ref.dtype)
        return carry

    lax.fori_loop(0, tc // CONV_ROWS, body, 0)


def _conv_module(u, w, b, g, beta, tc):
    S, C = u.shape
    per = tc // CONV_HALO
    return pl.pallas_call(
        _conv_kernel,
        grid=(S // tc,),
        in_specs=[
            pl.BlockSpec((tc, C), lambda i: (i, 0)),
            pl.BlockSpec((CONV_HALO, C), lambda i: (jnp.maximum(i * per - 1, 0), 0)),
            pl.BlockSpec((CONV_KERNEL, C), lambda i: (0, 0)),
            pl.BlockSpec((1, C), lambda i: (0, 0)),
            pl.BlockSpec((1, C), lambda i: (0, 0)),
            pl.BlockSpec((1, C), lambda i: (0, 0)),
        ],
        out_specs=pl.BlockSpec((tc, C), lambda i: (i, 0)),
        out_shape=jax.ShapeDtypeStruct((S, C), BF16),
        scratch_shapes=[pltpu.VMEM((CONV_HALO + tc, C), F32), pltpu.VMEM((tc, C), F32)],
        compiler_params=_params(1),
        name="conv_module",
    )(u, u, w, b.reshape(1, C), g.reshape(1, C), beta.reshape(1, C))


def _attn_kernel(q_ref, k_ref, v_ref, o_ref, kaug_ref, kmean_ref):
    i = pl.program_id(1)
    B = MOBA_BLOCK
    nb = k_ref.shape[0] // B
    lane = lax.broadcasted_iota(jnp.int32, (B, LANES), 1)

    @pl.when(i == 0)
    def _():
        kmean_ref[...] = jnp.zeros_like(kmean_ref)

        def blk(j, carry):
            off = pl.multiple_of(j * B, B)
            kb = k_ref[pl.ds(off, B), :]
            kaug_ref[pl.ds(off, B), 0:HEAD_DIM] = kb
            kaug_ref[pl.ds(off, B), HEAD_DIM:HEAD_DIM + LANES] = jnp.where(lane == j, 1.0, 0.0).astype(BF16)
            kmean_ref[pl.ds(j, 1), :] = jnp.sum(kb.astype(F32), axis=0, keepdims=True) * (1.0 / B)
            return carry

        lax.fori_loop(0, nb, blk, 0)

    q = q_ref[...]
    km_hi, km_lo = _split_bf16(kmean_ref[...])
    gate = (lax.dot_general(q, km_hi, _DN_T, preferred_element_type=F32)
            + lax.dot_general(q, km_lo, _DN_T, preferred_element_type=F32))
    neg_inf = -jnp.inf
    gate = jnp.where(lane < i, gate, neg_inf)
    sel = lane == i
    for _ in range(MOBA_TOP_K):
        m = jnp.max(gate, axis=1, keepdims=True)
        idx = jnp.min(jnp.where(gate == m, lane, LANES), axis=1, keepdims=True)
        hit = lane == idx
        sel = sel | (hit & (m > neg_inf))
        gate = jnp.where(hit, neg_inf, gate)
    penalty = jnp.where(sel, 0.0, MASK_PENALTY).astype(BF16)
    q_aug = jnp.concatenate([q, penalty], axis=1)

    off_i = pl.multiple_of(i * B, B)
    s = lax.dot_general(q_aug, kaug_ref[pl.ds(off_i, B), :], _DN_T, preferred_element_type=F32)
    row = lax.broadcasted_iota(jnp.int32, (B, B), 0)
    col = lax.broadcasted_iota(jnp.int32, (B, B), 1)
    s = jnp.where(col <= row, s, neg_inf)
    m0 = jnp.max(s, axis=1, keepdims=True)
    p = jnp.exp(s - m0)
    l0 = jnp.sum(p, axis=1, keepdims=True)
    acc0 = jnp.dot(p.astype(BF16), v_ref[pl.ds(off_i, B), :], preferred_element_type=F32)

    def body(j, carry):
        m, l, acc = carry
        off = pl.multiple_of(j * B, B)
        s = lax.dot_general(q_aug, kaug_ref[pl.ds(off, B), :], _DN_T, preferred_element_type=F32)
        m_new = jnp.maximum(m, jnp.max(s, axis=1, keepdims=True))
        alpha = jnp.exp(m - m_new)
        p = jnp.exp(s - m_new)
        l = alpha * l + jnp.sum(p, axis=1, keepdims=True)
        acc = alpha * acc + jnp.dot(p.astype(BF16), v_ref[pl.ds(off, B), :], preferred_element_type=F32)
        return m_new, l, acc

    m, l, acc = lax.fori_loop(0, i, body, (m0, l0, acc0))
    o_ref[...] = (acc / l).astype(o_ref.dtype)


def _moba_attention(qkv, n_heads):
    S = qkv.shape[0]
    assert S % MOBA_BLOCK == 0 and S // MOBA_BLOCK <= LANES
    B = MOBA_BLOCK
    return pl.pallas_call(
        _attn_kernel,
        grid=(n_heads, S // B),
        in_specs=[
            pl.BlockSpec((B, HEAD_DIM), lambda h, i: (i, h)),
            pl.BlockSpec((S, HEAD_DIM), lambda h, i: (0, n_heads + h)),
            pl.BlockSpec((S, HEAD_DIM), lambda h, i: (0, 2 * n_heads + h)),
        ],
        out_specs=pl.BlockSpec((B, HEAD_DIM), lambda h, i: (i, h)),
        out_shape=jax.ShapeDtypeStruct((S, n_heads * HEAD_DIM), BF16),
        scratch_shapes=[pltpu.VMEM((S, HEAD_DIM + LANES), BF16), pltpu.VMEM((LANES, HEAD_DIM), F32)],
        compiler_params=_params(2),
        name="moba_attention",
    )(qkv, qkv, qkv)


def _outproj_kernel(a_ref, u_ref, w_ref, h_ref, g_ref, b_ref, of_ref, ob_ref, *, alpha):
    da = a_ref.shape[1]
    mix = (jnp.dot(a_ref[...], w_ref[0:da, :], preferred_element_type=F32)
           + jnp.dot(u_ref[...], w_ref[da:, :], preferred_element_type=F32))
    y = _layer_norm(alpha * h_ref[...] + mix, g_ref[...], b_ref[...])
    of_ref[...] = y
    ob_ref[...] = y.astype(BF16)


def _out_proj_ln(attn, u, w, h, g, b, alpha, tm):
    S, D = h.shape
    da, du = attn.shape[1], u.shape[1]
    return pl.pallas_call(
        functools.partial(_outproj_kernel, alpha=alpha),
        grid=(S // tm,),
        in_specs=[
            pl.BlockSpec((tm, da), lambda i: (i, 0)),
            pl.BlockSpec((tm, du), lambda i: (i, 0)),
            pl.BlockSpec((da + du, D), lambda i: (0, 0)),
            pl.BlockSpec((tm, D), lambda i: (i, 0)),
            pl.BlockSpec((1, D), lambda i: (0, 0)),
            pl.BlockSpec((1, D), lambda i: (0, 0)),
        ],
        out_specs=[pl.BlockSpec((tm, D), lambda i: (i, 0)), pl.BlockSpec((tm, D), lambda i: (i, 0))],
        out_shape=[jax.ShapeDtypeStruct((S, D), F32), jax.ShapeDtypeStruct((S, D), BF16)],
        compiler_params=_params(1),
        name="out_proj_ln",
    )(attn, u, w, h, g.reshape(1, D), b.reshape(1, D))


def _swiglu_step(x, wg_ref, wu_ref, wd_ref, acc_ref):
    gg = jnp.dot(x, wg_ref[...], preferred_element_type=F32)
    uu = jnp.dot(x, wu_ref[...], preferred_element_type=F32)
    a = (gg * _sigmoid(gg) * uu).astype(BF16)
    acc_ref[...] += jnp.dot(a, wd_ref[...], preferred_element_type=F32)


def _ffn_dense_kernel(x_ref, wg_ref, wu_ref, wd_ref, h_ref, g_ref, b_ref, of_ref, ob_ref, acc_ref, *, alpha):
    f = pl.program_id(1)

    @pl.when(f == 0)
    def _():
        acc_ref[...] = jnp.zeros_like(acc_ref)

    _swiglu_step(x_ref[...], wg_ref, wu_ref, wd_ref, acc_ref)

    @pl.when(f == pl.num_programs(1) - 1)
    def _():
        y = _layer_norm(alpha * h_ref[...] + acc_ref[...], g_ref[...], b_ref[...])
        of_ref[...] = y
        ob_ref[...] = y.astype(BF16)


def _ffn_dense_ln(xb, wg, wu, wd, h, g, b, alpha, tm, tf):
    S, D = h.shape
    dff = wg.shape[1]
    return pl.pallas_call(
        functools.partial(_ffn_dense_kernel, alpha=alpha),
        grid=(S // tm, dff // tf),
        in_specs=[
            pl.BlockSpec((tm, D), lambda i, f: (i, 0)),
            pl.BlockSpec((D, tf), lambda i, f: (0, f)),
            pl.BlockSpec((D, tf), lambda i, f: (0, f)),
            pl.BlockSpec((tf, D), lambda i, f: (f, 0)),
            pl.BlockSpec((tm, D), lambda i, f: (i, 0)),
            pl.BlockSpec((1, D), lambda i, f: (0, 0)),
            pl.BlockSpec((1, D), lambda i, f: (0, 0)),
        ],
        out_specs=[pl.BlockSpec((tm, D), lambda i, f: (i, 0)), pl.BlockSpec((tm, D), lambda i, f: (i, 0))],
        out_shape=[jax.ShapeDtypeStruct((S, D), F32), jax.ShapeDtypeStruct((S, D), BF16)],
        scratch_shapes=[pltpu.VMEM((tm, D), F32)],
        compiler_params=_params(2),
        name="ffn_dense_ln",
    )(xb, wg, wu, wd, h, g.reshape(1, D), b.reshape(1, D))


def _ffn_grouped_kernel(te_ref, tv_ref, x_ref, wg_ref, wu_ref, wd_ref, y_ref, xb_ref, acc_ref):
    t = pl.program_id(0)
    f = pl.program_id(1)
    valid = tv_ref[t] == 1

    @pl.when(valid & (f == 0))
    def _():
        acc_ref[...] = jnp.zeros_like(acc_ref)
        xb_ref[...] = x_ref[...].astype(BF16)

    @pl.when(valid)
    def _():
        _swiglu_step(xb_ref[...], wg_ref, wu_ref, wd_ref, acc_ref)

    @pl.when(f == pl.num_programs(1) - 1)
    def _():
        y_ref[...] = jnp.where(valid, acc_ref[...], 0.0)


def _ffn_grouped(tile_expert, tile_valid, xperm, wg, wu, wd, tm, tf):
    R, D = xperm.shape
    dff = wg.shape[2]
    grid_spec = pltpu.PrefetchScalarGridSpec(
        num_scalar_prefetch=2,
        grid=(R // tm, dff // tf),
        in_specs=[
            pl.BlockSpec((tm, D), lambda t, f, te, tv: (t, 0)),
            pl.BlockSpec((None, D, tf), lambda t, f, te, tv: (te[t], 0, f)),
            pl.BlockSpec((None, D, tf), lambda t, f, te, tv: (te[t], 0, f)),
            pl.BlockSpec((None, tf, D), lambda t, f, te, tv: (te[t], f, 0)),
        ],
        out_specs=pl.BlockSpec((tm, D), lambda t, f, te, tv: (t, 0)),
        scratch_shapes=[pltpu.VMEM((tm, D), BF16), pltpu.VMEM((tm, D), F32)],
    )
    return pl.pallas_call(
        _ffn_grouped_kernel,
        grid_spec=grid_spec,
        out_shape=jax.ShapeDtypeStruct((R, D), F32),
        compiler_params=_params(2),
        name="ffn_grouped",
    )(tile_expert, tile_valid, xperm, wg, wu, wd)


def _router_kernel(h_ref, w_ref, meta_ref, cnt_ref, carry_ref, *, n_experts):
    i = pl.program_id(0)
    tr = h_ref.shape[0]

    @pl.when(i == 0)
    def _():
        carry_ref[...] = jnp.zeros_like(carry_ref)

    hh, hl = _split_bf16(h_ref[...])
    wh, wl = _split_bf16(w_ref[...])
    logits = (jnp.dot(hh, wh, preferred_element_type=F32) + jnp.dot(hh, wl, preferred_element_type=F32)
              + jnp.dot(hl, wh, preferred_element_type=F32))
    lane = lax.broadcasted_iota(jnp.int32, (tr, LANES), 1)
    neg_inf = -jnp.inf
    logits = jnp.where(lane < n_experts, logits, neg_inf)
    m0 = jnp.max(logits, axis=1, keepdims=True)
    i0 = jnp.min(jnp.where(logits == m0, lane, LANES), axis=1, keepdims=True)
    hit0 = lane == i0
    rest = jnp.where(hit0, neg_inf, logits)
    m1 = jnp.max(rest, axis=1, keepdims=True)
    i1 = jnp.min(jnp.where(rest == m1, lane, LANES), axis=1, keepdims=True)
    hit1 = lane == i1
    e = jnp.exp(m1 - m0)
    g0 = 1.0 / (1.0 + e)
    g1 = e / (1.0 + e)

    sel = jnp.where(hit0 | hit1, 1.0, 0.0)
    row = lax.broadcasted_iota(jnp.int32, (tr, tr), 0)
    col = lax.broadcasted_iota(jnp.int32, (tr, tr), 1)
    earlier = jnp.where(col < row, 1.0, 0.0).astype(BF16)
    before = jnp.dot(earlier, sel.astype(BF16), preferred_element_type=F32) + carry_ref[...]
    r0 = jnp.sum(jnp.where(hit0, before, 0.0), axis=1, keepdims=True)
    r1 = jnp.sum(jnp.where(hit1, before, 0.0), axis=1, keepdims=True)
    carry_ref[...] += jnp.sum(sel, axis=0, keepdims=True)

    cols = (i0.astype(F32), i1.astype(F32), g0, g1, r0, r1)
    meta = jnp.zeros((tr, LANES), F32)
    for c, val in enumerate(cols):
        meta = jnp.where(lane == c, val, meta)
    meta_ref[...] = meta
    cnt_ref[...] = jnp.broadcast_to(carry_ref[...], cnt_ref.shape)


def _router(h, w_pad, n_experts, tr):
    S, D = h.shape
    return pl.pallas_call(
        functools.partial(_router_kernel, n_experts=n_experts),
        grid=(S // tr,),
        in_specs=[pl.BlockSpec((tr, D), lambda i: (i, 0)), pl.BlockSpec((D, LANES), lambda i: (0, 0))],
        out_specs=[pl.BlockSpec((tr, LANES), lambda i: (i, 0)), pl.BlockSpec((8, LANES), lambda i: (0, 0))],
        out_shape=[jax.ShapeDtypeStruct((S, LANES), F32), jax.ShapeDtypeStruct((8, LANES), F32)],
        scratch_shapes=[pltpu.VMEM((1, LANES), F32)],
        compiler_params=_params(1),
        name="router",
    )(h, w_pad)


def _row_copy(src_ref, src_row, dst_ref, dst_row, sem):
    return pltpu.make_async_copy(src_ref.at[pl.ds(src_row, 1), :], dst_ref.at[pl.ds(dst_row, 1), :], sem)


def _dispatch_kernel(d0_ref, d1_ref, h_ref, xin_ref, xout_ref, sem):
    del xin_ref
    ts = h_ref.shape[0]
    base = pl.program_id(0) * ts

    def start(r, carry):
        _row_copy(h_ref, r, xout_ref, d0_ref[base + r], sem).start()
        _row_copy(h_ref, r, xout_ref, d1_ref[base + r], sem).start()
        return carry

    def wait(r, carry):
        _row_copy(h_ref, r, xout_ref, d0_ref[base + r], sem).wait()
        _row_copy(h_ref, r, xout_ref, d1_ref[base + r], sem).wait()
        return carry

    lax.fori_loop(0, ts, start, 0)
    lax.fori_loop(0, ts, wait, 0)


def _dispatch(d0, d1, h, n_rows, ts):
    S, D = h.shape
    grid_spec = pltpu.PrefetchScalarGridSpec(
        num_scalar_prefetch=2,
        grid=(S // ts,),
        in_specs=[pl.BlockSpec((ts, D), lambda i, a, b: (i, 0)), pl.BlockSpec(memory_space=pl.ANY)],
        out_specs=pl.BlockSpec(memory_space=pl.ANY),
        scratch_shapes=[pltpu.SemaphoreType.DMA(())],
    )
    return pl.pallas_call(
        _dispatch_kernel,
        grid_spec=grid_spec,
        out_shape=jax.ShapeDtypeStruct((n_rows, D), F32),
        input_output_aliases={3: 0},
        compiler_params=_params(1),
        name="moe_dispatch",
    )(d0, d1, h, jnp.zeros((n_rows, D), F32))


def _combine_kernel(d0_ref, d1_ref, meta_ref, h_ref, y_ref, g_ref, b_ref, of_ref, ob_ref,
                    y0_ref, y1_ref, sem, *, alpha):
    ts = h_ref.shape[0]
    base = pl.program_id(0) * ts

    def start(r, carry):
        _row_copy(y_ref, d0_ref[base + r], y0_ref, r, sem).start()
        _row_copy(y_ref, d1_ref[base + r], y1_ref, r, sem).start()
        return carry

    def wait(r, carry):
        _row_copy(y_ref, d0_ref[base + r], y0_ref, r, sem).wait()
        _row_copy(y_ref, d1_ref[base + r], y1_ref, r, sem).wait()
        return carry

    lax.fori_loop(0, ts, start, 0)
    lax.fori_loop(0, ts, wait, 0)
    meta = meta_ref[...]
    g0 = meta[:, 2:3]
    g1 = meta[:, 3:4]
    z = alpha * h_ref[...] + g0 * y0_ref[...] + g1 * y1_ref[...]
    y = _layer_norm(z, g_ref[...], b_ref[...])
    of_ref[...] = y
    ob_ref[...] = y.astype(BF16)


def _combine_ln(d0, d1, meta, h, y, g, b, alpha, ts):
    S, D = h.shape
    grid_spec = pltpu.PrefetchScalarGridSpec(
        num_scalar_prefetch=2,
        grid=(S // ts,),
        in_specs=[
            pl.BlockSpec((ts, LANES), lambda i, a, b: (i, 0)),
            pl.BlockSpec((ts, D), lambda i, a, b: (i, 0)),
            pl.BlockSpec(memory_space=pl.ANY),
            pl.BlockSpec((1, D), lambda i, a, b: (0, 0)),
            pl.BlockSpec((1, D), lambda i, a, b: (0, 0)),
        ],
        out_specs=[pl.BlockSpec((ts, D), lambda i, a, b: (i, 0)), pl.BlockSpec((ts, D), lambda i, a, b: (i, 0))],
        scratch_shapes=[pltpu.VMEM((ts, D), F32), pltpu.VMEM((ts, D), F32), pltpu.SemaphoreType.DMA(())],
    )
    return pl.pallas_call(
        functools.partial(_combine_kernel, alpha=alpha),
        grid_spec=grid_spec,
        out_shape=[jax.ShapeDtypeStruct((S, D), F32), jax.ShapeDtypeStruct((S, D), BF16)],
        compiler_params=_params(1),
        name="moe_combine_ln",
    )(d0, d1, meta, h, y, g.reshape(1, D), b.reshape(1, D))


def _tile(n, pref):
    t = min(n, pref)
    assert n % t == 0
    return t


def _moe_layer(hf, hb, router_w, wg, wu, wd, ln_g, ln_b, alpha):
    S, D = hf.shape
    E = wg.shape[0]
    tm = _tile(S, 512)
    w_pad = jnp.zeros((D, LANES), F32).at[:, :E].set(router_w)
    meta, cnt = _router(hf, w_pad, E, _tile(S, 256))
    counts = cnt[0, :E].astype(jnp.int32)
    tiles_per = (counts + tm - 1) // tm
    tile_end = jnp.cumsum(tiles_per)
    start = (tile_end - tiles_per) * tm
    n_tiles = TOP_K_EXPERTS * S // tm + E
    tile_ids = jnp.arange(n_tiles, dtype=jnp.int32)
    tile_expert = jnp.minimum(jnp.sum(tile_ids[:, None] >= tile_end[None, :], axis=1), E - 1).astype(jnp.int32)
    tile_valid = (tile_ids < tile_end[E - 1]).astype(jnp.int32)
    e0 = meta[:, 0].astype(jnp.int32)
    e1 = meta[:, 1].astype(jnp.int32)
    d0 = start[e0] + meta[:, 4].astype(jnp.int32)
    d1 = start[e1] + meta[:, 5].astype(jnp.int32)
    ts = _tile(S, 256)
    xperm = _dispatch(d0, d1, hf, n_tiles * tm, ts)
    y = _ffn_grouped(tile_expert, tile_valid, xperm, wg, wu, wd, tm, _tile(wg.shape[2], 512))
    return _combine_ln(d0, d1, meta, hf, y, ln_g, ln_b, alpha, ts)


def kernel(x, w_in, conv_dw_w, conv_dw_b, conv_ln_g, conv_ln_b, w_out, ln_mix_g, ln_mix_b, ln_ffn_g, ln_ffn_b,
           ffn_w_gate, ffn_w_up, ffn_w_down, router_w, expert_w_gate, expert_w_up, expert_w_down):
    batch, S, D = x.shape
    assert batch == 1
    depth = w_in.shape[0]
    conv_w = conv_dw_w.shape[2]
    attn_w = (w_in.shape[2] - 2 * conv_w) // 3
    n_heads = attn_w // HEAD_DIM
    alpha = (2.0 * depth) ** 0.25

    half = HEAD_DIM // 2
    inv_freq = jnp.power(ROPE_THETA, -jnp.arange(half, dtype=F32) / half)
    ang = jnp.arange(S, dtype=jnp.int32).astype(F32)[:, None] * inv_freq[None, :]
    cos = jnp.concatenate([jnp.cos(ang), jnp.cos(ang)], axis=-1)
    sin = jnp.concatenate([-jnp.sin(ang), jnp.sin(ang)], axis=-1)
    scale = HEAD_DIM ** -0.5
    cos_tab = jnp.stack([cos * scale, cos])
    sin_tab = jnp.stack([sin * scale, sin])

    hf = x[0]
    hb = hf.astype(BF16)
    tm_proj = _tile(S, 1024)
    for l in range(depth):
        w_in_b = w_in[l].astype(BF16)
        qkv = _qkv_proj(hb, w_in_b, cos_tab, sin_tab, attn_w, tm_proj, _tile(attn_w, 512))
        u = _glu_proj(hb, w_in_b, attn_w, conv_w, tm_proj, _tile(conv_w, 512))
        u = _conv_module(u, conv_dw_w[l], conv_dw_b[l], conv_ln_g[l], conv_ln_b[l], _tile(S, 512))
        attn = _moba_attention(qkv, n_heads)
        hf, hb = _out_proj_ln(attn, u, w_out[l].astype(BF16), hf, ln_mix_g[l], ln_mix_b[l], alpha, _tile(S, 512))
        if l % 2 == 0:
            j = l // 2
            hf, hb = _ffn_dense_ln(hb, ffn_w_gate[j].astype(BF16), ffn_w_up[j].astype(BF16),
                                   ffn_w_down[j].astype(BF16), hf, ln_ffn_g[l], ln_ffn_b[l], alpha,
                                   _tile(S, 512), _tile(ffn_w_gate.shape[2], 512))
        else:
            j = l // 2
            hf, hb = _moe_layer(hf, hb, router_w[j], expert_w_gate[j].astype(BF16), expert_w_up[j].astype(BF16),
                                expert_w_down[j].astype(BF16), ln_ffn_g[l], ln_ffn_b[l], alpha)
    return hf[None]
```

```python
import functools

import jax
import jax.numpy as jnp
from jax import lax
from jax.experimental import pallas as pl
from jax.experimental.pallas import tpu as pltpu

F32 = jnp.float32
BF16 = jnp.bfloat16

HEAD_DIM = 128
MOBA_BLOCK = 256
MOBA_TOP_K = 3
CONV_KERNEL = 31
ROPE_THETA = 10000.0
LN_EPS = 1e-5
TOP_K_EXPERTS = 2

LANES = 128
SUBLANES = 8
CONV_HALO = 32
CONV_ROWS = 32
MASK_PENALTY = -1e30
LOG2_E = 1.4426950408889634
ATTN_HEAD_GROUP = 4
BF16_ROWS = 16
ATTN_SUM_ROWS = BF16_ROWS
VMEM_LIMIT = 56 * 1024 * 1024

_DN_T = (((1,), (1,)), ((), ()))


def _params(n_axes):
    return pltpu.CompilerParams(dimension_semantics=("arbitrary",) * n_axes,
                                vmem_limit_bytes=VMEM_LIMIT)


def _sigmoid(x):
    return 1.0 / (1.0 + jnp.exp(-x))


def _layer_norm(z, g, b):
    mu = jnp.mean(z, axis=-1, keepdims=True)
    zc = z - mu
    var = jnp.mean(zc * zc, axis=-1, keepdims=True)
    return zc * lax.rsqrt(var + LN_EPS) * g + b


def _split_bf16(x):
    hi = x.astype(BF16)
    lo = (x - hi.astype(F32)).astype(BF16)
    return hi, lo


def _qkv_kernel(x_ref, w_ref, cos_ref, sin_ref, o_ref, *, n_rot_blocks):
    j = pl.program_id(1)
    acc = jnp.dot(x_ref[...], w_ref[...], preferred_element_type=F32)

    @pl.when(j < n_rot_blocks)
    def _():
        cos = cos_ref[0]
        sin = sin_ref[0]
        for c in range(0, acc.shape[1], HEAD_DIM):
            a = acc[:, c:c + HEAD_DIM]
            r = a * cos + pltpu.roll(a, HEAD_DIM // 2, 1) * sin
            o_ref[:, c:c + HEAD_DIM] = r.astype(o_ref.dtype)

    @pl.when(j >= n_rot_blocks)
    def _():
        o_ref[...] = acc.astype(o_ref.dtype)


def _qkv_proj(xb, w, cos_tab, sin_tab, attn_w, tm, tn):
    S, D = xb.shape
    n_sec = attn_w // tn
    return pl.pallas_call(
        functools.partial(_qkv_kernel, n_rot_blocks=2 * n_sec),
        grid=(S // tm, 3 * n_sec),
        in_specs=[
            pl.BlockSpec((tm, D), lambda i, j: (i, 0)),
            pl.BlockSpec((D, tn), lambda i, j: (0, j)),
            pl.BlockSpec((1, tm, HEAD_DIM), lambda i, j: (jnp.minimum(j // n_sec, 1), i, 0)),
            pl.BlockSpec((1, tm, HEAD_DIM), lambda i, j: (jnp.minimum(j // n_sec, 1), i, 0)),
        ],
        out_specs=pl.BlockSpec((tm, tn), lambda i, j: (i, j)),
        out_shape=jax.ShapeDtypeStruct((S, 3 * attn_w), BF16),
        compiler_params=_params(2),
        name="qkv_proj",
    )(xb, w, cos_tab, sin_tab)


def _glu_kernel(x_ref, wa_ref, wg_ref, o_ref):
    x = x_ref[...]
    a = jnp.dot(x, wa_ref[...], preferred_element_type=F32)
    g = jnp.dot(x, wg_ref[...], preferred_element_type=F32)
    o_ref[...] = a * _sigmoid(g)


def _glu_proj(xb, w, attn_w, conv_w, tm, tn):
    S, D = xb.shape
    a0 = 3 * attn_w // tn
    g0 = (3 * attn_w + conv_w) // tn
    return pl.pallas_call(
        _glu_kernel,
        grid=(S // tm, conv_w // tn),
        in_specs=[
            pl.BlockSpec((tm, D), lambda i, j: (i, 0)),
            pl.BlockSpec((D, tn), lambda i, j: (0, a0 + j)),
            pl.BlockSpec((D, tn), lambda i, j: (0, g0 + j)),
        ],
        out_specs=pl.BlockSpec((tm, tn), lambda i, j: (i, j)),
        out_shape=jax.ShapeDtypeStruct((S, conv_w), F32),
        compiler_params=_params(2),
        name="glu_proj",
    )(xb, w, w)


def _conv_kernel(u_ref, halo_ref, w_ref, b_ref, g_ref, beta_ref, o_ref, ext_ref, y_ref):
    i = pl.program_id(0)
    tc = u_ref.shape[0]
    ext_ref[0:CONV_HALO, :] = jnp.where(i > 0, halo_ref[...], 0.0)
    ext_ref[CONV_HALO:CONV_HALO + tc, :] = u_ref[...]
    bias = b_ref[...]
    g = g_ref[...]
    beta = beta_ref[...]
    first = CONV_HALO - (CONV_KERNEL - 1)

    def body(r, carry):
        base = pl.multiple_of(r * CONV_ROWS, CONV_ROWS)
        for c in range(0, u_ref.shape[1], LANES):
            win = ext_ref[pl.ds(base, 2 * CONV_ROWS), c:c + LANES]
            acc = jnp.zeros((CONV_ROWS, LANES), F32) + bias[:, c:c + LANES]
            for shift in range(SUBLANES):
                taps = [k for k in range(CONV_KERNEL) if (first + k) % SUBLANES == shift]
                shifted = win[shift:shift + 2 * CONV_ROWS - (SUBLANES if shift else 0)]
                for k in taps:
                    a = (first + k) // SUBLANES * SUBLANES
                    acc = acc + w_ref[k:k + 1, c:c + LANES] * shifted[a:a + CONV_ROWS]
            y_ref[pl.ds(base, CONV_ROWS), c:c + LANES] = acc
        y = _layer_norm(y_ref[pl.ds(base, CONV_ROWS), :], g, beta)
        o_ref[pl.ds(base, CONV_ROWS), :] = (y * _sigmoid(y)).astype(o_ref.dtype)
        return carry

    lax.fori_loop(0, tc // CONV_ROWS, body, 0)


def _conv_module(u, w, b, g, beta, tc):
    S, C = u.shape
    per = tc // CONV_HALO
    return pl.pallas_call(
        _conv_kernel,
        grid=(S // tc,),
        in_specs=[
            pl.BlockSpec((tc, C), lambda i: (i, 0)),
            pl.BlockSpec((CONV_HALO, C), lambda i: (jnp.maximum(i * per - 1, 0), 0)),
            pl.BlockSpec((CONV_KERNEL, C), lambda i: (0, 0)),
            pl.BlockSpec((1, C), lambda i: (0, 0)),
            pl.BlockSpec((1, C), lambda i: (0, 0)),
            pl.BlockSpec((1, C), lambda i: (0, 0)),
        ],
        out_specs=pl.BlockSpec((tc, C), lambda i: (i, 0)),
        out_shape=jax.ShapeDtypeStruct((S, C), BF16),
        scratch_shapes=[pltpu.VMEM((CONV_HALO + tc, C), F32), pltpu.VMEM((tc, C), F32)],
        compiler_params=_params(1),
        name="conv_module",
    )(u, u, w, b.reshape(1, C), g.reshape(1, C), beta.reshape(1, C))


def _attn_kernel(q_ref, k_ref, vt_ref, o_ref, kmean_ref, *scratch, n_group, n_gate):
    qa_refs = scratch[0:n_group]
    sa_refs = scratch[n_group:2 * n_group]
    sb_refs = scratch[2 * n_group:3 * n_group]
    acc_refs = scratch[3 * n_group:4 * n_group]
    i = pl.program_id(1)
    B = MOBA_BLOCK
    nb = k_ref.shape[0] // B
    gs = range(n_group)
    lane = lax.broadcasted_iota(jnp.int32, (B, LANES), 1)
    blk = lax.broadcasted_iota(jnp.int32, (n_gate, B), 0)
    ones_rows = jnp.ones((ATTN_SUM_ROWS, B), BF16)

    @pl.when(i == 0)
    def _():
        kmean_ref[...] = jnp.zeros_like(kmean_ref)

        def fill(j, carry):
            off = pl.multiple_of(j * B, B)
            for g in gs:
                kb = k_ref[pl.ds(off, B), g * HEAD_DIM:(g + 1) * HEAD_DIM]
                kmean_ref[g, pl.ds(j, 1), :] = jnp.sum(kb.astype(F32), axis=0, keepdims=True) * (1.0 / B)
            return carry

        lax.fori_loop(0, nb, fill, 0)

    neg_inf = -jnp.inf
    key_pos = lax.broadcasted_iota(jnp.int32, (B, B), 0)
    qry_pos = lax.broadcasted_iota(jnp.int32, (B, B), 1)
    off_i = pl.multiple_of(i * B, B)

    def cols(g):
        return slice(g * HEAD_DIM, (g + 1) * HEAD_DIM)

    def v_aug(g, j):
        return jnp.concatenate([vt_ref[j, cols(g), :], ones_rows], axis=0)

    def score(g, j, dst_ref):
        off = pl.multiple_of(j * B, B)
        onehot = jnp.where(lane == j, 1.0, 0.0).astype(BF16)
        k_cat = jnp.concatenate([k_ref[pl.ds(off, B), cols(g)], onehot], axis=1)
        dst_ref[...] = jnp.dot(k_cat, qa_refs[g][...], preferred_element_type=F32)

    def consume_all(j, src_refs, ms):
        ss = [src_refs[g][...] for g in gs]
        m_news = [jnp.maximum(ms[g], jnp.max(ss[g], axis=0, keepdims=True)) for g in gs]
        alphas = [jnp.exp2(ms[g] - m_news[g]) for g in gs]
        ps = [jnp.exp2(ss[g] - m_news[g]).astype(BF16) for g in gs]
        pvs = [jnp.dot(v_aug(g, j), ps[g], preferred_element_type=F32) for g in gs]
        for g in gs:
            acc_refs[g][...] = alphas[g] * acc_refs[g][...] + pvs[g]
        return tuple(m_news)

    q_ts = [q_ref[:, cols(g)].astype(F32).T.astype(BF16) for g in gs]
    kms = [_split_bf16(kmean_ref[g, 0:n_gate, :]) for g in gs]
    gates = [jnp.dot(kms[g][0], q_ts[g], preferred_element_type=F32)
             + jnp.dot(kms[g][1], q_ts[g], preferred_element_type=F32) for g in gs]
    gates = [jnp.where(blk < i, gates[g], neg_inf) for g in gs]
    sels = [blk < 0 for g in gs]
    for _ in range(MOBA_TOP_K):
        tops = [jnp.max(gates[g], axis=0, keepdims=True) for g in gs]
        idxs = [jnp.min(jnp.where(gates[g] == tops[g], blk, LANES), axis=0, keepdims=True) for g in gs]
        hits = [blk == idxs[g] for g in gs]
        sels = [sels[g] | (hits[g] & (tops[g] > neg_inf)) for g in gs]
        gates = [jnp.where(hits[g], neg_inf, gates[g]) for g in gs]
    pad = jnp.full((LANES - n_gate, B), MASK_PENALTY, BF16)
    for g in gs:
        penalty = jnp.where(sels[g], 0.0, MASK_PENALTY).astype(BF16)
        parts = [q_ts[g], penalty] + ([pad] if n_gate < LANES else [])
        qa_refs[g][...] = jnp.concatenate(parts, axis=0)

    ss = [jnp.dot(k_ref[pl.ds(off_i, B), cols(g)], q_ts[g], preferred_element_type=F32) for g in gs]
    ss = [jnp.where(key_pos <= qry_pos, ss[g], neg_inf) for g in gs]
    m0s = [jnp.max(ss[g], axis=0, keepdims=True) for g in gs]
    ps = [jnp.exp2(ss[g] - m0s[g]).astype(BF16) for g in gs]
    for g in gs:
        acc_refs[g][...] = jnp.dot(v_aug(g, i), ps[g], preferred_element_type=F32)
    for g in gs:
        score(g, 0, sa_refs[g])

    def body(pair, ms):
        j0 = 2 * pair
        j2 = jnp.minimum(j0 + 2, nb - 1)
        for g in gs:
            score(g, j0 + 1, sb_refs[g])
        ms = consume_all(j0, sa_refs, ms)
        for g in gs:
            score(g, j2, sa_refs[g])
        ms = consume_all(j0 + 1, sb_refs, ms)
        return ms

    lax.fori_loop(0, (i + 1) // 2, body, tuple(m0s))
    for g in gs:
        acc = acc_refs[g][...]
        out_t = acc[0:HEAD_DIM] / acc[HEAD_DIM:HEAD_DIM + 1]
        o_ref[:, cols(g)] = out_t.T.astype(o_ref.dtype)


def _moba_attention(qkv, v_t, n_heads, n_group):
    S = qkv.shape[0]
    assert S % MOBA_BLOCK == 0 and S // MOBA_BLOCK <= LANES and n_heads % n_group == 0
    B = MOBA_BLOCK
    nb = S // B
    gw = n_group * HEAD_DIM
    n_gate = min(LANES, -(-nb // BF16_ROWS) * BF16_ROWS)
    resident = pl.Buffered(1)
    scratch = ([pltpu.VMEM((HEAD_DIM + LANES, B), BF16)] * n_group
               + [pltpu.VMEM((B, B), F32)] * (2 * n_group)
               + [pltpu.VMEM((HEAD_DIM + ATTN_SUM_ROWS, B), F32)] * n_group)
    return pl.pallas_call(
        functools.partial(_attn_kernel, n_group=n_group, n_gate=n_gate),
        grid=(n_heads // n_group, nb),
        in_specs=[
            pl.BlockSpec((B, gw), lambda h, i: (i, h)),
            pl.BlockSpec((S, gw), lambda h, i: (0, n_heads // n_group + h), pipeline_mode=resident),
            pl.BlockSpec((nb, gw, B), lambda h, i: (0, h, 0), pipeline_mode=resident),
        ],
        out_specs=pl.BlockSpec((B, gw), lambda h, i: (i, h)),
        out_shape=jax.ShapeDtypeStruct((S, n_heads * HEAD_DIM), BF16),
        scratch_shapes=[pltpu.VMEM((n_group, LANES, HEAD_DIM), F32)] + scratch,
        compiler_params=_params(2),
        name="moba_attention",
    )(qkv, qkv, v_t)


def _outproj_kernel(a_ref, u_ref, w_ref, h_ref, g_ref, b_ref, of_ref, ob_ref, *, alpha):
    da = a_ref.shape[1]
    mix = (jnp.dot(a_ref[...], w_ref[0:da, :], preferred_element_type=F32)
           + jnp.dot(u_ref[...], w_ref[da:, :], preferred_element_type=F32))
    y = _layer_norm(alpha * h_ref[...] + mix, g_ref[...], b_ref[...])
    of_ref[...] = y
    ob_ref[...] = y.astype(BF16)


def _out_proj_ln(attn, u, w, h, g, b, alpha, tm):
    S, D = h.shape
    da, du = attn.shape[1], u.shape[1]
    return pl.pallas_call(
        functools.partial(_outproj_kernel, alpha=alpha),
        grid=(S // tm,),
        in_specs=[
            pl.BlockSpec((tm, da), lambda i: (i, 0)),
            pl.BlockSpec((tm, du), lambda i: (i, 0)),
            pl.BlockSpec((da + du, D), lambda i: (0, 0)),
            pl.BlockSpec((tm, D), lambda i: (i, 0)),
            pl.BlockSpec((1, D), lambda i: (0, 0)),
            pl.BlockSpec((1, D), lambda i: (0, 0)),
        ],
        out_specs=[pl.BlockSpec((tm, D), lambda i: (i, 0)), pl.BlockSpec((tm, D), lambda i: (i, 0))],
        out_shape=[jax.ShapeDtypeStruct((S, D), F32), jax.ShapeDtypeStruct((S, D), BF16)],
        compiler_params=_params(1),
        name="out_proj_ln",
    )(attn, u, w, h, g.reshape(1, D), b.reshape(1, D))


def _swiglu_step(x, wg_ref, wu_ref, wd_ref, acc_ref):
    gg = jnp.dot(x, wg_ref[...], preferred_element_type=F32)
    uu = jnp.dot(x, wu_ref[...], preferred_element_type=F32)
    a = (gg * _sigmoid(gg) * uu).astype(BF16)
    acc_ref[...] += jnp.dot(a, wd_ref[...], preferred_element_type=F32)


def _ffn_dense_kernel(x_ref, wg_ref, wu_ref, wd_ref, h_ref, g_ref, b_ref, of_ref, ob_ref, acc_ref, *, alpha):
    f = pl.program_id(1)

    @pl.when(f == 0)
    def _():
        acc_ref[...] = jnp.zeros_like(acc_ref)

    _swiglu_step(x_ref[...], wg_ref, wu_ref, wd_ref, acc_ref)

    @pl.when(f == pl.num_programs(1) - 1)
    def _():
        y = _layer_norm(alpha * h_ref[...] + acc_ref[...], g_ref[...], b_ref[...])
        of_ref[...] = y
        ob_ref[...] = y.astype(BF16)


def _ffn_dense_ln(xb, wg, wu, wd, h, g, b, alpha, tm, tf):
    S, D = h.shape
    dff = wg.shape[1]
    return pl.pallas_call(
        functools.partial(_ffn_dense_kernel, alpha=alpha),
        grid=(S // tm, dff // tf),
        in_specs=[
            pl.BlockSpec((tm, D), lambda i, f: (i, 0)),
            pl.BlockSpec((D, tf), lambda i, f: (0, f)),
            pl.BlockSpec((D, tf), lambda i, f: (0, f)),
            pl.BlockSpec((tf, D), lambda i, f: (f, 0)),
            pl.BlockSpec((tm, D), lambda i, f: (i, 0)),
            pl.BlockSpec((1, D), lambda i, f: (0, 0)),
            pl.BlockSpec((1, D), lambda i, f: (0, 0)),
        ],
        out_specs=[pl.BlockSpec((tm, D), lambda i, f: (i, 0)), pl.BlockSpec((tm, D), lambda i, f: (i, 0))],
        out_shape=[jax.ShapeDtypeStruct((S, D), F32), jax.ShapeDtypeStruct((S, D), BF16)],
        scratch_shapes=[pltpu.VMEM((tm, D), F32)],
        compiler_params=_params(2),
        name="ffn_dense_ln",
    )(xb, wg, wu, wd, h, g.reshape(1, D), b.reshape(1, D))


def _ffn_grouped_kernel(te_ref, tv_ref, x_ref, wg_ref, wu_ref, wd_ref, y_ref, xb_ref, acc_ref):
    t = pl.program_id(0)
    f = pl.program_id(1)
    valid = tv_ref[t] == 1

    @pl.when(valid & (f == 0))
    def _():
        acc_ref[...] = jnp.zeros_like(acc_ref)
        xb_ref[...] = x_ref[...].astype(BF16)

    @pl.when(valid)
    def _():
        _swiglu_step(xb_ref[...], wg_ref, wu_ref, wd_ref, acc_ref)

    @pl.when(f == pl.num_programs(1) - 1)
    def _():
        y_ref[...] = jnp.where(valid, acc_ref[...], 0.0)


def _ffn_grouped(tile_expert, tile_valid, xperm, wg, wu, wd, tm, tf):
    R, D = xperm.shape
    dff = wg.shape[2]
    grid_spec = pltpu.PrefetchScalarGridSpec(
        num_scalar_prefetch=2,
        grid=(R // tm, dff // tf),
        in_specs=[
            pl.BlockSpec((tm, D), lambda t, f, te, tv: (t, 0)),
            pl.BlockSpec((None, D, tf), lambda t, f, te, tv: (te[t], 0, f)),
            pl.BlockSpec((None, D, tf), lambda t, f, te, tv: (te[t], 0, f)),
            pl.BlockSpec((None, tf, D), lambda t, f, te, tv: (te[t], f, 0)),
        ],
        out_specs=pl.BlockSpec((tm, D), lambda t, f, te, tv: (t, 0)),
        scratch_shapes=[pltpu.VMEM((tm, D), BF16), pltpu.VMEM((tm, D), F32)],
    )
    return pl.pallas_call(
        _ffn_grouped_kernel,
        grid_spec=grid_spec,
        out_shape=jax.ShapeDtypeStruct((R, D), F32),
        compiler_params=_params(2),
        name="ffn_grouped",
    )(tile_expert, tile_valid, xperm, wg, wu, wd)


def _router_kernel(h_ref, w_ref, meta_ref, cnt_ref, carry_ref, *, n_experts):
    i = pl.program_id(0)
    tr = h_ref.shape[0]

    @pl.when(i == 0)
    def _():
        carry_ref[...] = jnp.zeros_like(carry_ref)

    hh, hl = _split_bf16(h_ref[...])
    wh, wl = _split_bf16(w_ref[...])
    logits = (jnp.dot(hh, wh, preferred_element_type=F32) + jnp.dot(hh, wl, preferred_element_type=F32)
              + jnp.dot(hl, wh, preferred_element_type=F32))
    lane = lax.broadcasted_iota(jnp.int32, (tr, LANES), 1)
    neg_inf = -jnp.inf
    logits = jnp.where(lane < n_experts, logits, neg_inf)
    m0 = jnp.max(logits, axis=1, keepdims=True)
    i0 = jnp.min(jnp.where(logits == m0, lane, LANES), axis=1, keepdims=True)
    hit0 = lane == i0
    rest = jnp.where(hit0, neg_inf, logits)
    m1 = jnp.max(rest, axis=1, keepdims=True)
    i1 = jnp.min(jnp.where(rest == m1, lane, LANES), axis=1, keepdims=True)
    hit1 = lane == i1
    e = jnp.exp(m1 - m0)
    g0 = 1.0 / (1.0 + e)
    g1 = e / (1.0 + e)

    sel = jnp.where(hit0 | hit1, 1.0, 0.0)
    row = lax.broadcasted_iota(jnp.int32, (tr, tr), 0)
    col = lax.broadcasted_iota(jnp.int32, (tr, tr), 1)
    earlier = jnp.where(col < row, 1.0, 0.0).astype(BF16)
    before = jnp.dot(earlier, sel.astype(BF16), preferred_element_type=F32) + carry_ref[...]
    r0 = jnp.sum(jnp.where(hit0, before, 0.0), axis=1, keepdims=True)
    r1 = jnp.sum(jnp.where(hit1, before, 0.0), axis=1, keepdims=True)
    carry_ref[...] += jnp.sum(sel, axis=0, keepdims=True)

    cols = (i0.astype(F32), i1.astype(F32), g0, g1, r0, r1)
    meta = jnp.zeros((tr, LANES), F32)
    for c, val in enumerate(cols):
        meta = jnp.where(lane == c, val, meta)
    meta_ref[...] = meta
    cnt_ref[...] = jnp.broadcast_to(carry_ref[...], cnt_ref.shape)


def _router(h, w_pad, n_experts, tr):
    S, D = h.shape
    return pl.pallas_call(
        functools.partial(_router_kernel, n_experts=n_experts),
        grid=(S // tr,),
        in_specs=[pl.BlockSpec((tr, D), lambda i: (i, 0)), pl.BlockSpec((D, LANES), lambda i: (0, 0))],
        out_specs=[pl.BlockSpec((tr, LANES), lambda i: (i, 0)), pl.BlockSpec((8, LANES), lambda i: (0, 0))],
        out_shape=[jax.ShapeDtypeStruct((S, LANES), F32), jax.ShapeDtypeStruct((8, LANES), F32)],
        scratch_shapes=[pltpu.VMEM((1, LANES), F32)],
        compiler_params=_params(1),
        name="router",
    )(h, w_pad)


def _row_copy(src_ref, src_row, dst_ref, dst_row, sem):
    return pltpu.make_async_copy(src_ref.at[pl.ds(src_row, 1), :], dst_ref.at[pl.ds(dst_row, 1), :], sem)


def _dispatch_kernel(d0_ref, d1_ref, h_ref, xin_ref, xout_ref, sem):
    del xin_ref
    ts = h_ref.shape[0]
    base = pl.program_id(0) * ts

    def start(r, carry):
        _row_copy(h_ref, r, xout_ref, d0_ref[base + r], sem).start()
        _row_copy(h_ref, r, xout_ref, d1_ref[base + r], sem).start()
        return carry

    def wait(r, carry):
        _row_copy(h_ref, r, xout_ref, d0_ref[base + r], sem).wait()
        _row_copy(h_ref, r, xout_ref, d1_ref[base + r], sem).wait()
        return carry

    lax.fori_loop(0, ts, start, 0)
    lax.fori_loop(0, ts, wait, 0)


def _dispatch(d0, d1, h, n_rows, ts):
    S, D = h.shape
    grid_spec = pltpu.PrefetchScalarGridSpec(
        num_scalar_prefetch=2,
        grid=(S // ts,),
        in_specs=[pl.BlockSpec((ts, D), lambda i, a, b: (i, 0)), pl.BlockSpec(memory_space=pl.ANY)],
        out_specs=pl.BlockSpec(memory_space=pl.ANY),
        scratch_shapes=[pltpu.SemaphoreType.DMA(())],
    )
    return pl.pallas_call(
        _dispatch_kernel,
        grid_spec=grid_spec,
        out_shape=jax.ShapeDtypeStruct((n_rows, D), F32),
        input_output_aliases={3: 0},
        compiler_params=_params(1),
        name="moe_dispatch",
    )(d0, d1, h, jnp.zeros((n_rows, D), F32))


def _combine_kernel(d0_ref, d1_ref, meta_ref, h_ref, y_ref, g_ref, b_ref, of_ref, ob_ref,
                    y0_ref, y1_ref, sem, *, alpha):
    ts = h_ref.shape[0]
    base = pl.program_id(0) * ts

    def start(r, carry):
        _row_copy(y_ref, d0_ref[base + r], y0_ref, r, sem).start()
        _row_copy(y_ref, d1_ref[base + r], y1_ref, r, sem).start()
        return carry

    def wait(r, carry):
        _row_copy(y_ref, d0_ref[base + r], y0_ref, r, sem).wait()
        _row_copy(y_ref, d1_ref[base + r], y1_ref, r, sem).wait()
        return carry

    lax.fori_loop(0, ts, start, 0)
    lax.fori_loop(0, ts, wait, 0)
    meta = meta_ref[...]
    g0 = meta[:, 2:3]
    g1 = meta[:, 3:4]
    z = alpha * h_ref[...] + g0 * y0_ref[...] + g1 * y1_ref[...]
    y = _layer_norm(z, g_ref[...], b_ref[...])
    of_ref[...] = y
    ob_ref[...] = y.astype(BF16)


def _combine_ln(d0, d1, meta, h, y, g, b, alpha, ts):
    S, D = h.shape
    grid_spec = pltpu.PrefetchScalarGridSpec(
        num_scalar_prefetch=2,
        grid=(S // ts,),
        in_specs=[
            pl.BlockSpec((ts, LANES), lambda i, a, b: (i, 0)),
            pl.BlockSpec((ts, D), lambda i, a, b: (i, 0)),
            pl.BlockSpec(memory_space=pl.ANY),
            pl.BlockSpec((1, D), lambda i, a, b: (0, 0)),
            pl.BlockSpec((1, D), lambda i, a, b: (0, 0)),
        ],
        out_specs=[pl.BlockSpec((ts, D), lambda i, a, b: (i, 0)), pl.BlockSpec((ts, D), lambda i, a, b: (i, 0))],
        scratch_shapes=[pltpu.VMEM((ts, D), F32), pltpu.VMEM((ts, D), F32), pltpu.SemaphoreType.DMA(())],
    )
    return pl.pallas_call(
        functools.partial(_combine_kernel, alpha=alpha),
        grid_spec=grid_spec,
        out_shape=[jax.ShapeDtypeStruct((S, D), F32), jax.ShapeDtypeStruct((S, D), BF16)],
        compiler_params=_params(1),
        name="moe_combine_ln",
    )(d0, d1, meta, h, y, g.reshape(1, D), b.reshape(1, D))


def _tile(n, pref):
    t = min(n, pref)
    assert n % t == 0
    return t


def _moe_layer(hf, hb, router_w, wg, wu, wd, ln_g, ln_b, alpha):
    S, D = hf.shape
    E = wg.shape[0]
    tm = _tile(S, 512)
    w_pad = jnp.zeros((D, LANES), F32).at[:, :E].set(router_w)
    meta, cnt = _router(hf, w_pad, E, _tile(S, 256))
    counts = cnt[0, :E].astype(jnp.int32)
    tiles_per = (counts + tm - 1) // tm
    tile_end = jnp.cumsum(tiles_per)
    start = (tile_end - tiles_per) * tm
    n_tiles = TOP_K_EXPERTS * S // tm + E
    tile_ids = jnp.arange(n_tiles, dtype=jnp.int32)
    tile_expert = jnp.minimum(jnp.sum(tile_ids[:, None] >= tile_end[None, :], axis=1), E - 1).astype(jnp.int32)
    tile_valid = (tile_ids < tile_end[E - 1]).astype(jnp.int32)
    e0 = meta[:, 0].astype(jnp.int32)
    e1 = meta[:, 1].astype(jnp.int32)
    d0 = start[e0] + meta[:, 4].astype(jnp.int32)
    d1 = start[e1] + meta[:, 5].astype(jnp.int32)
    ts = _tile(S, 256)
    xperm = _dispatch(d0, d1, hf, n_tiles * tm, ts)
    y = _ffn_grouped(tile_expert, tile_valid, xperm, wg, wu, wd, tm, _tile(wg.shape[2], 512))
    return _combine_ln(d0, d1, meta, hf, y, ln_g, ln_b, alpha, ts)


def kernel(x, w_in, conv_dw_w, conv_dw_b, conv_ln_g, conv_ln_b, w_out, ln_mix_g, ln_mix_b, ln_ffn_g, ln_ffn_b,
           ffn_w_gate, ffn_w_up, ffn_w_down, router_w, expert_w_gate, expert_w_up, expert_w_down):
    batch, S, D = x.shape
    assert batch == 1
    depth = w_in.shape[0]
    conv_w = conv_dw_w.shape[2]
    attn_w = (w_in.shape[2] - 2 * conv_w) // 3
    n_heads = attn_w // HEAD_DIM
    alpha = (2.0 * depth) ** 0.25

    half = HEAD_DIM // 2
    inv_freq = jnp.power(ROPE_THETA, -jnp.arange(half, dtype=F32) / half)
    ang = jnp.arange(S, dtype=jnp.int32).astype(F32)[:, None] * inv_freq[None, :]
    cos = jnp.concatenate([jnp.cos(ang), jnp.cos(ang)], axis=-1)
    sin = jnp.concatenate([-jnp.sin(ang), jnp.sin(ang)], axis=-1)
    scale = HEAD_DIM ** -0.5 * LOG2_E
    cos_tab = jnp.stack([cos * scale, cos])
    sin_tab = jnp.stack([sin * scale, sin])

    hf = x[0]
    hb = hf.astype(BF16)
    tm_proj = _tile(S, 1024)
    for l in range(depth):
        w_in_b = w_in[l].astype(BF16)
        qkv = _qkv_proj(hb, w_in_b, cos_tab, sin_tab, attn_w, tm_proj, _tile(attn_w, 512))
        u = _glu_proj(hb, w_in_b, attn_w, conv_w, tm_proj, _tile(conv_w, 512))
        u = _conv_module(u, conv_dw_w[l], conv_dw_b[l], conv_ln_g[l], conv_ln_b[l], _tile(S, 512))
        v_t = qkv[:, 2 * attn_w:].reshape(S // MOBA_BLOCK, MOBA_BLOCK, attn_w).transpose(0, 2, 1)
        attn = _moba_attention(qkv, v_t, n_heads, min(n_heads, ATTN_HEAD_GROUP))
        hf, hb = _out_proj_ln(attn, u, w_out[l].astype(BF16), hf, ln_mix_g[l], ln_mix_b[l], alpha, _tile(S, 512))
        if l % 2 == 0:
            j = l // 2
            hf, hb = _ffn_dense_ln(hb, ffn_w_gate[j].astype(BF16), ffn_w_up[j].astype(BF16),
                                   ffn_w_down[j].astype(BF16), hf, ln_ffn_g[l], ln_ffn_b[l], alpha,
                                   _tile(S, 512), _tile(ffn_w_gate.shape[2], 512))
        else:
            j = l // 2
            hf, hb = _moe_layer(hf, hb, router_w[j], expert_w_gate[j].astype(BF16), expert_w_up[j].astype(BF16),
                                expert_w_down[j].astype(BF16), ln_ffn_g[l], ln_ffn_b[l], alpha)
    return hf[None]
```

```python
import functools

import jax
import jax.numpy as jnp
from jax import lax
from jax.experimental import pallas as pl
from jax.experimental.pallas import tpu as pltpu

F32 = jnp.float32
BF16 = jnp.bfloat16

HEAD_DIM = 128
MOBA_BLOCK = 256
MOBA_TOP_K = 3
CONV_KERNEL = 31
ROPE_THETA = 10000.0
LN_EPS = 1e-5
TOP_K_EXPERTS = 2

LANES = 128
SUBLANES = 8
CONV_HALO = 32
CONV_ROWS = 32
MASK_PENALTY = -1e30
LOG2_E = 1.4426950408889634
ATTN_HEAD_GROUP = 4
BF16_ROWS = 16
ATTN_SUM_ROWS = BF16_ROWS
VMEM_LIMIT = 56 * 1024 * 1024

_DN_T = (((1,), (1,)), ((), ()))


def _params(n_axes):
    return pltpu.CompilerParams(dimension_semantics=("arbitrary",) * n_axes,
                                vmem_limit_bytes=VMEM_LIMIT)


def _sigmoid(x):
    return 1.0 / (1.0 + jnp.exp(-x))


def _layer_norm(z, g, b):
    mu = jnp.mean(z, axis=-1, keepdims=True)
    zc = z - mu
    var = jnp.mean(zc * zc, axis=-1, keepdims=True)
    return zc * lax.rsqrt(var + LN_EPS) * g + b


def _split_bf16(x):
    hi = x.astype(BF16)
    lo = (x - hi.astype(F32)).astype(BF16)
    return hi, lo


def _qkv_kernel(x_ref, w_ref, cos_ref, sin_ref, o_ref, *, n_rot_blocks):
    j = pl.program_id(1)
    acc = jnp.dot(x_ref[...], w_ref[...], preferred_element_type=F32)

    @pl.when(j < n_rot_blocks)
    def _():
        cos = cos_ref[0]
        sin = sin_ref[0]
        for c in range(0, acc.shape[1], HEAD_DIM):
            a = acc[:, c:c + HEAD_DIM]
            r = a * cos + pltpu.roll(a, HEAD_DIM // 2, 1) * sin
            o_ref[:, c:c + HEAD_DIM] = r.astype(o_ref.dtype)

    @pl.when(j >= n_rot_blocks)
    def _():
        o_ref[...] = acc.astype(o_ref.dtype)


def _qkv_proj(xb, w, cos_tab, sin_tab, attn_w, tm, tn):
    S, D = xb.shape
    n_sec = attn_w // tn
    return pl.pallas_call(
        functools.partial(_qkv_kernel, n_rot_blocks=2 * n_sec),
        grid=(S // tm, 3 * n_sec),
        in_specs=[
            pl.BlockSpec((tm, D), lambda i, j: (i, 0)),
            pl.BlockSpec((D, tn), lambda i, j: (0, j)),
            pl.BlockSpec((1, tm, HEAD_DIM), lambda i, j: (jnp.minimum(j // n_sec, 1), i, 0)),
            pl.BlockSpec((1, tm, HEAD_DIM), lambda i, j: (jnp.minimum(j // n_sec, 1), i, 0)),
        ],
        out_specs=pl.BlockSpec((tm, tn), lambda i, j: (i, j)),
        out_shape=jax.ShapeDtypeStruct((S, 3 * attn_w), BF16),
        compiler_params=_params(2),
        name="qkv_proj",
    )(xb, w, cos_tab, sin_tab)


def _glu_kernel(x_ref, wa_ref, wg_ref, o_ref):
    x = x_ref[...]
    a = jnp.dot(x, wa_ref[...], preferred_element_type=F32)
    g = jnp.dot(x, wg_ref[...], preferred_element_type=F32)
    o_ref[...] = a * _sigmoid(g)


def _glu_proj(xb, w, attn_w, conv_w, tm, tn):
    S, D = xb.shape
    a0 = 3 * attn_w // tn
    g0 = (3 * attn_w + conv_w) // tn
    return pl.pallas_call(
        _glu_kernel,
        grid=(S // tm, conv_w // tn),
        in_specs=[
            pl.BlockSpec((tm, D), lambda i, j: (i, 0)),
            pl.BlockSpec((D, tn), lambda i, j: (0, a0 + j)),
            pl.BlockSpec((D, tn), lambda i, j: (0, g0 + j)),
        ],
        out_specs=pl.BlockSpec((tm, tn), lambda i, j: (i, j)),
        out_shape=jax.ShapeDtypeStruct((S, conv_w), F32),
        compiler_params=_params(2),
        name="glu_proj",
    )(xb, w, w)


def _conv_kernel(u_ref, halo_ref, w_ref, b_ref, g_ref, beta_ref, o_ref, ext_ref, y_ref):
    i = pl.program_id(0)
    tc = u_ref.shape[0]
    ext_ref[0:CONV_HALO, :] = jnp.where(i > 0, halo_ref[...], 0.0)
    ext_ref[CONV_HALO:CONV_HALO + tc, :] = u_ref[...]
    bias = b_ref[...]
    g = g_ref[...]
    beta = beta_ref[...]
    first = CONV_HALO - (CONV_KERNEL - 1)

    def body(r, carry):
        base = pl.multiple_of(r * CONV_ROWS, CONV_ROWS)
        for c in range(0, u_ref.shape[1], LANES):
            win = ext_ref[pl.ds(base, 2 * CONV_ROWS), c:c + LANES]
            acc = jnp.zeros((CONV_ROWS, LANES), F32) + bias[:, c:c + LANES]
            for shift in range(SUBLANES):
                taps = [k for k in range(CONV_KERNEL) if (first + k) % SUBLANES == shift]
                shifted = win[shift:shift + 2 * CONV_ROWS - (SUBLANES if shift else 0)]
                for k in taps:
                    a = (first + k) // SUBLANES * SUBLANES
                    acc = acc + w_ref[k:k + 1, c:c + LANES] * shifted[a:a + CONV_ROWS]
            y_ref[pl.ds(base, CONV_ROWS), c:c + LANES] = acc
        y = _layer_norm(y_ref[pl.ds(base, CONV_ROWS), :], g, beta)
        o_ref[pl.ds(base, CONV_ROWS), :] = (y * _sigmoid(y)).astype(o_ref.dtype)
        return carry

    lax.fori_loop(0, tc // CONV_ROWS, body, 0)


def _conv_module(u, w, b, g, beta, tc):
    S, C = u.shape
    per = tc // CONV_HALO
    return pl.pallas_call(
        _conv_kernel,
        grid=(S // tc,),
        in_specs=[
            pl.BlockSpec((tc, C), lambda i: (i, 0)),
            pl.BlockSpec((CONV_HALO, C), lambda i: (jnp.maximum(i * per - 1, 0), 0)),
            pl.BlockSpec((CONV_KERNEL, C), lambda i: (0, 0)),
            pl.BlockSpec((1, C), lambda i: (0, 0)),
            pl.BlockSpec((1, C), lambda i: (0, 0)),
            pl.BlockSpec((1, C), lambda i: (0, 0)),
        ],
        out_specs=pl.BlockSpec((tc, C), lambda i: (i, 0)),
        out_shape=jax.ShapeDtypeStruct((S, C), BF16),
        scratch_shapes=[pltpu.VMEM((CONV_HALO + tc, C), F32), pltpu.VMEM((tc, C), F32)],
        compiler_params=_params(1),
        name="conv_module",
    )(u, u, w, b.reshape(1, C), g.reshape(1, C), beta.reshape(1, C))


def _attn_kernel(q_ref, k_ref, vt_ref, o_ref, kmean_ref, *scratch, n_group, n_gate):
    qa_refs = scratch[0:n_group]
    sa_refs = scratch[n_group:2 * n_group]
    sb_refs = scratch[2 * n_group:3 * n_group]
    acc_refs = scratch[3 * n_group:4 * n_group]
    i = pl.program_id(1)
    B = MOBA_BLOCK
    nb = k_ref.shape[0] // B
    gs = range(n_group)
    lane = lax.broadcasted_iota(jnp.int32, (B, LANES), 1)
    blk = lax.broadcasted_iota(jnp.int32, (n_gate, B), 0)
    ones_rows = jnp.ones((ATTN_SUM_ROWS, B), BF16)

    @pl.when(i == 0)
    def _():
        kmean_ref[...] = jnp.zeros_like(kmean_ref)

        def fill(j, carry):
            off = pl.multiple_of(j * B, B)
            for g in gs:
                kb = k_ref[pl.ds(off, B), g * HEAD_DIM:(g + 1) * HEAD_DIM]
                kmean_ref[g, pl.ds(j, 1), :] = jnp.sum(kb.astype(F32), axis=0, keepdims=True) * (1.0 / B)
            return carry

        lax.fori_loop(0, nb, fill, 0)

    neg_inf = -jnp.inf
    key_pos = lax.broadcasted_iota(jnp.int32, (B, B), 0)
    qry_pos = lax.broadcasted_iota(jnp.int32, (B, B), 1)
    off_i = pl.multiple_of(i * B, B)

    def cols(g):
        return slice(g * HEAD_DIM, (g + 1) * HEAD_DIM)

    def v_aug(g, j):
        return jnp.concatenate([vt_ref[j, cols(g), :], ones_rows], axis=0)

    def score(g, j, dst_ref):
        off = pl.multiple_of(j * B, B)
        onehot = jnp.where(lane == j, 1.0, 0.0).astype(BF16)
        k_cat = jnp.concatenate([k_ref[pl.ds(off, B), cols(g)], onehot], axis=1)
        dst_ref[...] = jnp.dot(k_cat, qa_refs[g][...], preferred_element_type=F32)

    def consume_all(j, src_refs, ms):
        ss = [src_refs[g][...] for g in gs]
        m_news = [jnp.maximum(ms[g], jnp.max(ss[g], axis=0, keepdims=True)) for g in gs]
        alphas = [jnp.exp2(ms[g] - m_news[g]) for g in gs]
        ps = [jnp.exp2(ss[g] - m_news[g]).astype(BF16) for g in gs]
        pvs = [jnp.dot(v_aug(g, j), ps[g], preferred_element_type=F32) for g in gs]
        for g in gs:
            acc_refs[g][...] = alphas[g] * acc_refs[g][...] + pvs[g]
        return tuple(m_news)

    q_ts = [q_ref[:, cols(g)].astype(F32).T.astype(BF16) for g in gs]
    kms = [_split_bf16(kmean_ref[g, 0:n_gate, :]) for g in gs]
    gates = [jnp.dot(kms[g][0], q_ts[g], preferred_element_type=F32)
             + jnp.dot(kms[g][1], q_ts[g], preferred_element_type=F32) for g in gs]
    gates = [jnp.where(blk < i, gates[g], neg_inf) for g in gs]
    sels = [blk < 0 for g in gs]
    for _ in range(MOBA_TOP_K):
        tops = [jnp.max(gates[g], axis=0, keepdims=True) for g in gs]
        idxs = [jnp.min(jnp.where(gates[g] == tops[g], blk, LANES), axis=0, keepdims=True) for g in gs]
        hits = [blk == idxs[g] for g in gs]
        sels = [sels[g] | (hits[g] & (tops[g] > neg_inf)) for g in gs]
        gates = [jnp.where(hits[g], neg_inf, gates[g]) for g in gs]
    pad = jnp.full((LANES - n_gate, B), MASK_PENALTY, BF16)
    for g in gs:
        penalty = jnp.where(sels[g], 0.0, MASK_PENALTY).astype(BF16)
        parts = [q_ts[g], penalty] + ([pad] if n_gate < LANES else [])
        qa_refs[g][...] = jnp.concatenate(parts, axis=0)

    ss = [jnp.dot(k_ref[pl.ds(off_i, B), cols(g)], q_ts[g], preferred_element_type=F32) for g in gs]
    ss = [jnp.where(key_pos <= qry_pos, ss[g], neg_inf) for g in gs]
    m0s = [jnp.max(ss[g], axis=0, keepdims=True) for g in gs]
    ps = [jnp.exp2(ss[g] - m0s[g]).astype(BF16) for g in gs]
    for g in gs:
        acc_refs[g][...] = jnp.dot(v_aug(g, i), ps[g], preferred_element_type=F32)
    for g in gs:
        score(g, 0, sa_refs[g])

    def body(pair, ms):
        j0 = 2 * pair
        j2 = jnp.minimum(j0 + 2, nb - 1)
        for g in gs:
            score(g, j0 + 1, sb_refs[g])
        ms = consume_all(j0, sa_refs, ms)
        for g in gs:
            score(g, j2, sa_refs[g])
        ms = consume_all(j0 + 1, sb_refs, ms)
        return ms

    lax.fori_loop(0, (i + 1) // 2, body, tuple(m0s))
    for g in gs:
        acc = acc_refs[g][...]
        out_t = acc[0:HEAD_DIM] / acc[HEAD_DIM:HEAD_DIM + 1]
        o_ref[:, cols(g)] = out_t.T.astype(o_ref.dtype)


def _moba_attention(qkv, v_t, n_heads, n_group):
    S = qkv.shape[0]
    assert S % MOBA_BLOCK == 0 and S // MOBA_BLOCK <= LANES and n_heads % n_group == 0
    B = MOBA_BLOCK
    nb = S // B
    gw = n_group * HEAD_DIM
    n_gate = min(LANES, -(-nb // BF16_ROWS) * BF16_ROWS)
    resident = pl.Buffered(1)
    scratch = ([pltpu.VMEM((HEAD_DIM + LANES, B), BF16)] * n_group
               + [pltpu.VMEM((B, B), F32)] * (2 * n_group)
               + [pltpu.VMEM((HEAD_DIM + ATTN_SUM_ROWS, B), F32)] * n_group)
    return pl.pallas_call(
        functools.partial(_attn_kernel, n_group=n_group, n_gate=n_gate),
        grid=(n_heads // n_group, nb),
        in_specs=[
            pl.BlockSpec((B, gw), lambda h, i: (i, h)),
            pl.BlockSpec((S, gw), lambda h, i: (0, n_heads // n_group + h), pipeline_mode=resident),
            pl.BlockSpec((nb, gw, B), lambda h, i: (0, h, 0), pipeline_mode=resident),
        ],
        out_specs=pl.BlockSpec((B, gw), lambda h, i: (i, h)),
        out_shape=jax.ShapeDtypeStruct((S, n_heads * HEAD_DIM), BF16),
        scratch_shapes=[pltpu.VMEM((n_group, LANES, HEAD_DIM), F32)] + scratch,
        compiler_params=_params(2),
        name="moba_attention",
    )(qkv, qkv, v_t)


def _outproj_kernel(a_ref, u_ref, w_ref, h_ref, g_ref, b_ref, of_ref, ob_ref, *, alpha):
    da = a_ref.shape[1]
    mix = (jnp.dot(a_ref[...], w_ref[0:da, :], preferred_element_type=F32)
           + jnp.dot(u_ref[...], w_ref[da:, :], preferred_element_type=F32))
    y = _layer_norm(alpha * h_ref[...] + mix, g_ref[...], b_ref[...])
    of_ref[...] = y
    ob_ref[...] = y.astype(BF16)


def _out_proj_ln(attn, u, w, h, g, b, alpha, tm):
    S, D = h.shape
    da, du = attn.shape[1], u.shape[1]
    return pl.pallas_call(
        functools.partial(_outproj_kernel, alpha=alpha),
        grid=(S // tm,),
        in_specs=[
            pl.BlockSpec((tm, da), lambda i: (i, 0)),
            pl.BlockSpec((tm, du), lambda i: (i, 0)),
            pl.BlockSpec((da + du, D), lambda i: (0, 0)),
            pl.BlockSpec((tm, D), lambda i: (i, 0)),
            pl.BlockSpec((1, D), lambda i: (0, 0)),
            pl.BlockSpec((1, D), lambda i: (0, 0)),
        ],
        out_specs=[pl.BlockSpec((tm, D), lambda i: (i, 0)), pl.BlockSpec((tm, D), lambda i: (i, 0))],
        out_shape=[jax.ShapeDtypeStruct((S, D), F32), jax.ShapeDtypeStruct((S, D), BF16)],
        compiler_params=_params(1),
        name="out_proj_ln",
    )(attn, u, w, h, g.reshape(1, D), b.reshape(1, D))


def _swiglu_step(x, wg_ref, wu_ref, wd_ref, acc_ref):
    gg = jnp.dot(x, wg_ref[...], preferred_element_type=F32)
    uu = jnp.dot(x, wu_ref[...], preferred_element_type=F32)
    a = (gg * _sigmoid(gg) * uu).astype(BF16)
    acc_ref[...] += jnp.dot(a, wd_ref[...], preferred_element_type=F32)


def _ffn_dense_kernel(x_ref, wg_ref, wu_ref, wd_ref, h_ref, g_ref, b_ref, *rest, alpha, n_cast):
    cast_in = rest[:n_cast]
    of_ref, ob_ref = rest[n_cast:n_cast + 2]
    cast_out = rest[n_cast + 2:2 * n_cast + 2]
    acc_ref = rest[2 * n_cast + 2]
    f = pl.program_id(1)

    @pl.when(f == 0)
    def _():
        acc_ref[...] = jnp.zeros_like(acc_ref)

    _swiglu_step(x_ref[...], wg_ref, wu_ref, wd_ref, acc_ref)

    for src, dst in zip(cast_in, cast_out):
        dst[...] = src[...].astype(dst.dtype)

    @pl.when(f == pl.num_programs(1) - 1)
    def _():
        y = _layer_norm(alpha * h_ref[...] + acc_ref[...], g_ref[...], b_ref[...])
        of_ref[...] = y
        ob_ref[...] = y.astype(BF16)


def _side_cast_spec(shape, ni, nf):
    E, rows, ncol = shape
    if ncol % (nf * LANES) == 0 and rows % (ni * BF16_ROWS) == 0:
        return pl.BlockSpec((E, rows // ni, ncol // nf), lambda i, f: (0, i, f))
    if rows % (ni * nf * BF16_ROWS) == 0:
        return pl.BlockSpec((E, rows // (ni * nf), ncol), lambda i, f: (0, i * nf + f, 0))
    return None


def _ffn_dense_ln(xb, wg, wu, wd, h, g, b, alpha, tm, tf, expert_w=()):
    S, D = h.shape
    dff = wg.shape[1]
    ni, nf = S // tm, dff // tf
    cast_specs = [_side_cast_spec(w.shape, ni, nf) for w in expert_w]
    n_cast = len(expert_w)
    outs = pl.pallas_call(
        functools.partial(_ffn_dense_kernel, alpha=alpha, n_cast=n_cast),
        grid=(ni, nf),
        in_specs=[
            pl.BlockSpec((tm, D), lambda i, f: (i, 0)),
            pl.BlockSpec((D, tf), lambda i, f: (0, f)),
            pl.BlockSpec((D, tf), lambda i, f: (0, f)),
            pl.BlockSpec((tf, D), lambda i, f: (f, 0)),
            pl.BlockSpec((tm, D), lambda i, f: (i, 0)),
            pl.BlockSpec((1, D), lambda i, f: (0, 0)),
            pl.BlockSpec((1, D), lambda i, f: (0, 0)),
        ] + cast_specs,
        out_specs=[pl.BlockSpec((tm, D), lambda i, f: (i, 0)), pl.BlockSpec((tm, D), lambda i, f: (i, 0))] + cast_specs,
        out_shape=[jax.ShapeDtypeStruct((S, D), F32), jax.ShapeDtypeStruct((S, D), BF16)]
        + [jax.ShapeDtypeStruct(w.shape, BF16) for w in expert_w],
        scratch_shapes=[pltpu.VMEM((tm, D), F32)],
        compiler_params=_params(2),
        name="ffn_dense_ln",
    )(xb, wg, wu, wd, h, g.reshape(1, D), b.reshape(1, D), *expert_w)
    return outs[0], outs[1], tuple(outs[2:])


def _ffn_grouped_kernel(te_ref, tv_ref, x_ref, wg_ref, wu_ref, wd_ref, y_ref, xb_ref, acc_ref):
    t = pl.program_id(0)
    f = pl.program_id(1)
    valid = tv_ref[t] == 1

    @pl.when(valid & (f == 0))
    def _():
        acc_ref[...] = jnp.zeros_like(acc_ref)
        xb_ref[...] = x_ref[...].astype(BF16)

    @pl.when(valid)
    def _():
        _swiglu_step(xb_ref[...], wg_ref, wu_ref, wd_ref, acc_ref)

    @pl.when(f == pl.num_programs(1) - 1)
    def _():
        y_ref[...] = jnp.where(valid, acc_ref[...], 0.0)


def _ffn_grouped(tile_expert, tile_valid, xperm, wg, wu, wd, tm, tf):
    R, D = xperm.shape
    dff = wg.shape[2]
    nf = dff // tf

    def chunk(t, f, tv):
        return jnp.where(tv[t] == 1, f, nf - 1)

    grid_spec = pltpu.PrefetchScalarGridSpec(
        num_scalar_prefetch=2,
        grid=(R // tm, nf),
        in_specs=[
            pl.BlockSpec((tm, D), lambda t, f, te, tv: (t, 0)),
            pl.BlockSpec((None, D, tf), lambda t, f, te, tv: (te[t], 0, chunk(t, f, tv))),
            pl.BlockSpec((None, D, tf), lambda t, f, te, tv: (te[t], 0, chunk(t, f, tv))),
            pl.BlockSpec((None, tf, D), lambda t, f, te, tv: (te[t], chunk(t, f, tv), 0)),
        ],
        out_specs=pl.BlockSpec((tm, D), lambda t, f, te, tv: (t, 0)),
        scratch_shapes=[pltpu.VMEM((tm, D), BF16), pltpu.VMEM((tm, D), F32)],
    )
    return pl.pallas_call(
        _ffn_grouped_kernel,
        grid_spec=grid_spec,
        out_shape=jax.ShapeDtypeStruct((R, D), F32),
        compiler_params=_params(2),
        name="ffn_grouped",
    )(tile_expert, tile_valid, xperm, wg, wu, wd)


def _router_kernel(h_ref, w_ref, meta_ref, cnt_ref, carry_ref, *, n_experts):
    i = pl.program_id(0)
    tr = h_ref.shape[0]

    @pl.when(i == 0)
    def _():
        carry_ref[...] = jnp.zeros_like(carry_ref)

    hh, hl = _split_bf16(h_ref[...])
    wh, wl = _split_bf16(w_ref[...])
    logits = (jnp.dot(hh, wh, preferred_element_type=F32) + jnp.dot(hh, wl, preferred_element_type=F32)
              + jnp.dot(hl, wh, preferred_element_type=F32))
    lane = lax.broadcasted_iota(jnp.int32, (tr, LANES), 1)
    neg_inf = -jnp.inf
    logits = jnp.where(lane < n_experts, logits, neg_inf)
    m0 = jnp.max(logits, axis=1, keepdims=True)
    i0 = jnp.min(jnp.where(logits == m0, lane, LANES), axis=1, keepdims=True)
    hit0 = lane == i0
    rest = jnp.where(hit0, neg_inf, logits)
    m1 = jnp.max(rest, axis=1, keepdims=True)
    i1 = jnp.min(jnp.where(rest == m1, lane, LANES), axis=1, keepdims=True)
    hit1 = lane == i1
    e = jnp.exp(m1 - m0)
    g0 = 1.0 / (1.0 + e)
    g1 = e / (1.0 + e)

    sel = jnp.where(hit0 | hit1, 1.0, 0.0)
    row = lax.broadcasted_iota(jnp.int32, (tr, tr), 0)
    col = lax.broadcasted_iota(jnp.int32, (tr, tr), 1)
    earlier = jnp.where(col < row, 1.0, 0.0).astype(BF16)
    before = jnp.dot(earlier, sel.astype(BF16), preferred_element_type=F32) + carry_ref[...]
    r0 = jnp.sum(jnp.where(hit0, before, 0.0), axis=1, keepdims=True)
    r1 = jnp.sum(jnp.where(hit1, before, 0.0), axis=1, keepdims=True)
    carry_ref[...] += jnp.sum(sel, axis=0, keepdims=True)

    cols = (i0.astype(F32), i1.astype(F32), g0, g1, r0, r1)
    meta = jnp.zeros((tr, LANES), F32)
    for c, val in enumerate(cols):
        meta = jnp.where(lane == c, val, meta)
    meta_ref[...] = meta
    cnt_ref[...] = jnp.broadcast_to(carry_ref[...], cnt_ref.shape)


def _router(h, w_pad, n_experts, tr):
    S, D = h.shape
    return pl.pallas_call(
        functools.partial(_router_kernel, n_experts=n_experts),
        grid=(S // tr,),
        in_specs=[pl.BlockSpec((tr, D), lambda i: (i, 0)), pl.BlockSpec((D, LANES), lambda i: (0, 0))],
        out_specs=[pl.BlockSpec((tr, LANES), lambda i: (i, 0)), pl.BlockSpec((8, LANES), lambda i: (0, 0))],
        out_shape=[jax.ShapeDtypeStruct((S, LANES), F32), jax.ShapeDtypeStruct((8, LANES), F32)],
        scratch_shapes=[pltpu.VMEM((1, LANES), F32)],
        compiler_params=_params(1),
        name="router",
    )(h, w_pad)


def _row_copy(src_ref, src_row, dst_ref, dst_row, sem):
    return pltpu.make_async_copy(src_ref.at[pl.ds(src_row, 1), :], dst_ref.at[pl.ds(dst_row, 1), :], sem)


def _dispatch_kernel(d0_ref, d1_ref, h_ref, xin_ref, xout_ref, buf_ref, sem):
    del xin_ref
    i = pl.program_id(0)
    n = pl.num_programs(0)
    ts = h_ref.shape[0]
    slot = i % 2

    def rows(step, s, act):
        base = step * ts

        def body(r, carry):
            act(_row_copy(buf_ref.at[s], r, xout_ref, d0_ref[base + r], sem.at[s]))
            act(_row_copy(buf_ref.at[s], r, xout_ref, d1_ref[base + r], sem.at[s]))
            return carry

        lax.fori_loop(0, ts, body, 0)

    @pl.when(i >= 2)
    def _():
        rows(i - 2, slot, lambda c: c.wait())

    buf_ref[slot] = h_ref[...]
    rows(i, slot, lambda c: c.start())

    @pl.when(i == n - 1)
    def _():
        @pl.when(i >= 1)
        def _():
            rows(i - 1, 1 - slot, lambda c: c.wait())

        rows(i, slot, lambda c: c.wait())


def _dispatch(d0, d1, h, n_rows, ts):
    S, D = h.shape
    grid_spec = pltpu.PrefetchScalarGridSpec(
        num_scalar_prefetch=2,
        grid=(S // ts,),
        in_specs=[pl.BlockSpec((ts, D), lambda i, a, b: (i, 0)), pl.BlockSpec(memory_space=pl.ANY)],
        out_specs=pl.BlockSpec(memory_space=pl.ANY),
        scratch_shapes=[pltpu.VMEM((2, ts, D), F32), pltpu.SemaphoreType.DMA((2,))],
    )
    return pl.pallas_call(
        _dispatch_kernel,
        grid_spec=grid_spec,
        out_shape=jax.ShapeDtypeStruct((n_rows, D), F32),
        input_output_aliases={3: 0},
        compiler_params=_params(1),
        name="moe_dispatch",
    )(d0, d1, h, jnp.zeros((n_rows, D), F32))


def _combine_kernel(d0_ref, d1_ref, meta_ref, h_ref, y_ref, g_ref, b_ref, of_ref, ob_ref,
                    y0_ref, y1_ref, sem, *, alpha):
    i = pl.program_id(0)
    n = pl.num_programs(0)
    ts = h_ref.shape[0]
    slot = i % 2

    def rows(step, s, act):
        base = step * ts

        def body(r, carry):
            act(_row_copy(y_ref, d0_ref[base + r], y0_ref.at[s], r, sem.at[s]))
            act(_row_copy(y_ref, d1_ref[base + r], y1_ref.at[s], r, sem.at[s]))
            return carry

        lax.fori_loop(0, ts, body, 0)

    @pl.when(i == 0)
    def _():
        rows(0, 0, lambda c: c.start())

    @pl.when(i + 1 < n)
    def _():
        rows(i + 1, 1 - slot, lambda c: c.start())

    rows(i, slot, lambda c: c.wait())
    meta = meta_ref[...]
    g0 = meta[:, 2:3]
    g1 = meta[:, 3:4]
    z = alpha * h_ref[...] + g0 * y0_ref[slot] + g1 * y1_ref[slot]
    y = _layer_norm(z, g_ref[...], b_ref[...])
    of_ref[...] = y
    ob_ref[...] = y.astype(BF16)


def _combine_ln(d0, d1, meta, h, y, g, b, alpha, ts):
    S, D = h.shape
    grid_spec = pltpu.PrefetchScalarGridSpec(
        num_scalar_prefetch=2,
        grid=(S // ts,),
        in_specs=[
            pl.BlockSpec((ts, LANES), lambda i, a, b: (i, 0)),
            pl.BlockSpec((ts, D), lambda i, a, b: (i, 0)),
            pl.BlockSpec(memory_space=pl.ANY),
            pl.BlockSpec((1, D), lambda i, a, b: (0, 0)),
            pl.BlockSpec((1, D), lambda i, a, b: (0, 0)),
        ],
        out_specs=[pl.BlockSpec((ts, D), lambda i, a, b: (i, 0)), pl.BlockSpec((ts, D), lambda i, a, b: (i, 0))],
        scratch_shapes=[pltpu.VMEM((2, ts, D), F32), pltpu.VMEM((2, ts, D), F32), pltpu.SemaphoreType.DMA((2,))],
    )
    return pl.pallas_call(
        functools.partial(_combine_kernel, alpha=alpha),
        grid_spec=grid_spec,
        out_shape=[jax.ShapeDtypeStruct((S, D), F32), jax.ShapeDtypeStruct((S, D), BF16)],
        compiler_params=_params(1),
        name="moe_combine_ln",
    )(d0, d1, meta, h, y, g.reshape(1, D), b.reshape(1, D))


def _tile(n, pref):
    t = min(n, pref)
    assert n % t == 0
    return t


def _moe_layer(hf, hb, router_w, wg, wu, wd, ln_g, ln_b, alpha):
    S, D = hf.shape
    E = wg.shape[0]
    tm = _tile(S, 512)
    w_pad = jnp.zeros((D, LANES), F32).at[:, :E].set(router_w)
    meta, cnt = _router(hf, w_pad, E, _tile(S, 256))
    counts = cnt[0, :E].astype(jnp.int32)
    tiles_per = (counts + tm - 1) // tm
    tile_end = jnp.cumsum(tiles_per)
    start = (tile_end - tiles_per) * tm
    n_tiles = TOP_K_EXPERTS * S // tm + E
    tile_ids = jnp.arange(n_tiles, dtype=jnp.int32)
    tile_expert = jnp.minimum(jnp.sum(tile_ids[:, None] >= tile_end[None, :], axis=1), E - 1).astype(jnp.int32)
    tile_valid = (tile_ids < tile_end[E - 1]).astype(jnp.int32)
    e0 = meta[:, 0].astype(jnp.int32)
    e1 = meta[:, 1].astype(jnp.int32)
    d0 = start[e0] + meta[:, 4].astype(jnp.int32)
    d1 = start[e1] + meta[:, 5].astype(jnp.int32)
    ts = _tile(S, 256)
    xperm = _dispatch(d0, d1, hf, n_tiles * tm, ts)
    y = _ffn_grouped(tile_expert, tile_valid, xperm, wg, wu, wd, tm, _tile(wg.shape[2], 512))
    return _combine_ln(d0, d1, meta, hf, y, ln_g, ln_b, alpha, ts)


def kernel(x, w_in, conv_dw_w, conv_dw_b, conv_ln_g, conv_ln_b, w_out, ln_mix_g, ln_mix_b, ln_ffn_g, ln_ffn_b,
           ffn_w_gate, ffn_w_up, ffn_w_down, router_w, expert_w_gate, expert_w_up, expert_w_down):
    batch, S, D = x.shape
    assert batch == 1
    depth = w_in.shape[0]
    conv_w = conv_dw_w.shape[2]
    attn_w = (w_in.shape[2] - 2 * conv_w) // 3
    n_heads = attn_w // HEAD_DIM
    alpha = (2.0 * depth) ** 0.25

    half = HEAD_DIM // 2
    inv_freq = jnp.power(ROPE_THETA, -jnp.arange(half, dtype=F32) / half)
    ang = jnp.arange(S, dtype=jnp.int32).astype(F32)[:, None] * inv_freq[None, :]
    cos = jnp.concatenate([jnp.cos(ang), jnp.cos(ang)], axis=-1)
    sin = jnp.concatenate([-jnp.sin(ang), jnp.sin(ang)], axis=-1)
    scale = HEAD_DIM ** -0.5 * LOG2_E
    cos_tab = jnp.stack([cos * scale, cos])
    sin_tab = jnp.stack([sin * scale, sin])

    hf = x[0]
    hb = hf.astype(BF16)
    tm_proj = _tile(S, 1024)
    for l in range(depth):
        w_in_b = w_in[l].astype(BF16)
        qkv = _qkv_proj(hb, w_in_b, cos_tab, sin_tab, attn_w, tm_proj, _tile(attn_w, 1024))
        u = _glu_proj(hb, w_in_b, attn_w, conv_w, tm_proj, _tile(conv_w, 1024))
        u = _conv_module(u, conv_dw_w[l], conv_dw_b[l], conv_ln_g[l], conv_ln_b[l], _tile(S, 512))
        v_t = qkv[:, 2 * attn_w:].reshape(S // MOBA_BLOCK, MOBA_BLOCK, attn_w).transpose(0, 2, 1)
        attn = _moba_attention(qkv, v_t, n_heads, min(n_heads, ATTN_HEAD_GROUP))
        hf, hb = _out_proj_ln(attn, u, w_out[l].astype(BF16), hf, ln_mix_g[l], ln_mix_b[l], alpha, _tile(S, 512))
        j = l // 2
        if l % 2 == 0:
            tm, tf = _tile(S, 512), _tile(ffn_w_gate.shape[2], 512)
            side = (expert_w_gate[j], expert_w_up[j], expert_w_down[j]) if l + 1 < depth else ()
            if any(_side_cast_spec(w.shape, S // tm, ffn_w_gate.shape[2] // tf) is None for w in side):
                side = ()
            hf, hb, experts_b = _ffn_dense_ln(hb, ffn_w_gate[j].astype(BF16), ffn_w_up[j].astype(BF16),
                                              ffn_w_down[j].astype(BF16), hf, ln_ffn_g[l], ln_ffn_b[l], alpha,
                                              tm, tf, side)
        else:
            if not experts_b:
                experts_b = (expert_w_gate[j].astype(BF16), expert_w_up[j].astype(BF16), expert_w_down[j].astype(BF16))
            hf, hb = _moe_layer(hf, hb, router_w[j], *experts_b, ln_ffn_g[l], ln_ffn_b[l], alpha)
    return hf[None]
```

```python
import functools

import jax
import jax.numpy as jnp
from jax import lax
from jax.experimental import pallas as pl
from jax.experimental.pallas import tpu as pltpu

F32 = jnp.float32
BF16 = jnp.bfloat16

HEAD_DIM = 128
MOBA_BLOCK = 256
MOBA_TOP_K = 3
CONV_KERNEL = 31
ROPE_THETA = 10000.0
LN_EPS = 1e-5
TOP_K_EXPERTS = 2

LANES = 128
SUBLANES = 8
CONV_HALO = 32
CONV_ROWS = 32
MASK_PENALTY = -1e30
LOG2_E = 1.4426950408889634
ATTN_HEAD_GROUP = 4
GROUPED_FF_CHUNK = 512
ATTN_UNROLL = 4
BF16_ROWS = 16
ATTN_SUM_ROWS = BF16_ROWS
VMEM_LIMIT = 56 * 1024 * 1024

_DN_T = (((1,), (1,)), ((), ()))


def _params(n_axes):
    return pltpu.CompilerParams(dimension_semantics=("arbitrary",) * n_axes,
                                vmem_limit_bytes=VMEM_LIMIT)


def _sigmoid(x):
    return 1.0 / (1.0 + jnp.exp(-x))


def _layer_norm(z, g, b):
    mu = jnp.mean(z, axis=-1, keepdims=True)
    zc = z - mu
    var = jnp.mean(zc * zc, axis=-1, keepdims=True)
    return zc * lax.rsqrt(var + LN_EPS) * g + b


def _split_bf16(x):
    hi = x.astype(BF16)
    lo = (x - hi.astype(F32)).astype(BF16)
    return hi, lo


def _qkv_kernel(x_ref, w_ref, cos_ref, sin_ref, o_ref, *, n_rot_blocks):
    j = pl.program_id(1)
    acc = jnp.dot(x_ref[...], w_ref[...], preferred_element_type=F32)

    @pl.when(j < n_rot_blocks)
    def _():
        cos = cos_ref[0]
        sin = sin_ref[0]
        for c in range(0, acc.shape[1], HEAD_DIM):
            a = acc[:, c:c + HEAD_DIM]
            r = a * cos + pltpu.roll(a, HEAD_DIM // 2, 1) * sin
            o_ref[:, c:c + HEAD_DIM] = r.astype(o_ref.dtype)

    @pl.when(j >= n_rot_blocks)
    def _():
        o_ref[...] = acc.astype(o_ref.dtype)


def _qkv_proj(xb, w, cos_tab, sin_tab, attn_w, tm, tn):
    S, D = xb.shape
    n_sec = attn_w // tn
    return pl.pallas_call(
        functools.partial(_qkv_kernel, n_rot_blocks=2 * n_sec),
        grid=(S // tm, 3 * n_sec),
        in_specs=[
            pl.BlockSpec((tm, D), lambda i, j: (i, 0)),
            pl.BlockSpec((D, tn), lambda i, j: (0, j)),
            pl.BlockSpec((1, tm, HEAD_DIM), lambda i, j: (jnp.minimum(j // n_sec, 1), i, 0)),
            pl.BlockSpec((1, tm, HEAD_DIM), lambda i, j: (jnp.minimum(j // n_sec, 1), i, 0)),
        ],
        out_specs=pl.BlockSpec((tm, tn), lambda i, j: (i, j)),
        out_shape=jax.ShapeDtypeStruct((S, 3 * attn_w), BF16),
        compiler_params=_params(2),
        name="qkv_proj",
    )(xb, w, cos_tab, sin_tab)


def _glu_kernel(x_ref, wa_ref, wg_ref, o_ref):
    x = x_ref[...]
    a = jnp.dot(x, wa_ref[...], preferred_element_type=F32)
    g = jnp.dot(x, wg_ref[...], preferred_element_type=F32)
    o_ref[...] = a * _sigmoid(g)


def _glu_proj(xb, w, attn_w, conv_w, tm, tn):
    S, D = xb.shape
    a0 = 3 * attn_w // tn
    g0 = (3 * attn_w + conv_w) // tn
    return pl.pallas_call(
        _glu_kernel,
        grid=(S // tm, conv_w // tn),
        in_specs=[
            pl.BlockSpec((tm, D), lambda i, j: (i, 0)),
            pl.BlockSpec((D, tn), lambda i, j: (0, a0 + j)),
            pl.BlockSpec((D, tn), lambda i, j: (0, g0 + j)),
        ],
        out_specs=pl.BlockSpec((tm, tn), lambda i, j: (i, j)),
        out_shape=jax.ShapeDtypeStruct((S, conv_w), F32),
        compiler_params=_params(2),
        name="glu_proj",
    )(xb, w, w)


def _conv_kernel(u_ref, halo_ref, w_ref, b_ref, g_ref, beta_ref, o_ref, ext_ref, y_ref):
    i = pl.program_id(0)
    tc = u_ref.shape[0]
    ext_ref[0:CONV_HALO, :] = jnp.where(i > 0, halo_ref[...], 0.0)
    ext_ref[CONV_HALO:CONV_HALO + tc, :] = u_ref[...]
    bias = b_ref[...]
    g = g_ref[...]
    beta = beta_ref[...]
    first = CONV_HALO - (CONV_KERNEL - 1)
    n_acc = CONV_ROWS // SUBLANES
    sub = lax.broadcasted_iota(jnp.int32, (SUBLANES, LANES), 0)

    def body(r, carry):
        base = pl.multiple_of(r * CONV_ROWS, CONV_ROWS)
        for c in range(0, u_ref.shape[1], LANES):
            win = ext_ref[pl.ds(base, 2 * CONV_ROWS), c:c + LANES]
            groups = [win[v:v + SUBLANES] for v in range(0, 2 * CONV_ROWS, SUBLANES)]
            accs = [jnp.zeros((SUBLANES, LANES), F32) + bias[:, c:c + LANES]] * n_acc
            for shift in range(SUBLANES):
                taps = [k for k in range(CONV_KERNEL) if (first + k) % SUBLANES == shift]
                if shift == 0:
                    shifted = groups
                else:
                    shifted = [pltpu.roll(jnp.where(sub >= shift, groups[v], groups[v + 1]), SUBLANES - shift, 0)
                               for v in range(len(groups) - 1)]
                for k in taps:
                    a = (first + k) // SUBLANES
                    wk = w_ref[k:k + 1, c:c + LANES]
                    accs = [accs[v] + wk * shifted[a + v] for v in range(n_acc)]
            y_ref[pl.ds(base, CONV_ROWS), c:c + LANES] = jnp.concatenate(accs, axis=0)
        y = _layer_norm(y_ref[pl.ds(base, CONV_ROWS), :], g, beta)
        o_ref[pl.ds(base, CONV_ROWS), :] = (y * _sigmoid(y)).astype(o_ref.dtype)
        return carry

    lax.fori_loop(0, tc // CONV_ROWS, body, 0)


def _conv_module(u, w, b, g, beta, tc):
    S, C = u.shape
    per = tc // CONV_HALO
    return pl.pallas_call(
        _conv_kernel,
        grid=(S // tc,),
        in_specs=[
            pl.BlockSpec((tc, C), lambda i: (i, 0)),
            pl.BlockSpec((CONV_HALO, C), lambda i: (jnp.maximum(i * per - 1, 0), 0)),
            pl.BlockSpec((CONV_KERNEL, C), lambda i: (0, 0)),
            pl.BlockSpec((1, C), lambda i: (0, 0)),
            pl.BlockSpec((1, C), lambda i: (0, 0)),
            pl.BlockSpec((1, C), lambda i: (0, 0)),
        ],
        out_specs=pl.BlockSpec((tc, C), lambda i: (i, 0)),
        out_shape=jax.ShapeDtypeStruct((S, C), BF16),
        scratch_shapes=[pltpu.VMEM((CONV_HALO + tc, C), F32), pltpu.VMEM((tc, C), F32)],
        compiler_params=_params(1),
        name="conv_module",
    )(u, u, w, b.reshape(1, C), g.reshape(1, C), beta.reshape(1, C))


def _attn_kernel(q_ref, k_ref, vt_ref, o_ref, kmean_ref, *scratch, n_group, n_gate):
    qa_refs = scratch[0:n_group]
    sa_refs = scratch[n_group:2 * n_group]
    sb_refs = scratch[2 * n_group:3 * n_group]
    acc_refs = scratch[3 * n_group:4 * n_group]
    i = pl.program_id(1)
    B = MOBA_BLOCK
    nb = k_ref.shape[0] // B
    gs = range(n_group)
    lane = lax.broadcasted_iota(jnp.int32, (B, LANES), 1)
    blk = lax.broadcasted_iota(jnp.int32, (n_gate, B), 0)
    ones_rows = jnp.ones((ATTN_SUM_ROWS, B), BF16)

    @pl.when(i == 0)
    def _():
        kmean_ref[...] = jnp.zeros_like(kmean_ref)

        def fill(j, carry):
            off = pl.multiple_of(j * B, B)
            for g in gs:
                kb = k_ref[pl.ds(off, B), g * HEAD_DIM:(g + 1) * HEAD_DIM]
                kmean_ref[g, pl.ds(j, 1), :] = jnp.sum(kb.astype(F32), axis=0, keepdims=True) * (1.0 / B)
            return carry

        lax.fori_loop(0, nb, fill, 0)

    neg_inf = -jnp.inf
    key_pos = lax.broadcasted_iota(jnp.int32, (B, B), 0)
    qry_pos = lax.broadcasted_iota(jnp.int32, (B, B), 1)
    off_i = pl.multiple_of(i * B, B)

    def cols(g):
        return slice(g * HEAD_DIM, (g + 1) * HEAD_DIM)

    def v_aug(g, j):
        return jnp.concatenate([vt_ref[j, cols(g), :], ones_rows], axis=0)

    def score(g, j, dst_ref):
        off = pl.multiple_of(j * B, B)
        onehot = jnp.where(lane == j, 1.0, 0.0).astype(BF16)
        k_cat = jnp.concatenate([k_ref[pl.ds(off, B), cols(g)], onehot], axis=1)
        dst_ref[...] = jnp.dot(k_cat, qa_refs[g][...], preferred_element_type=F32)

    def consume_all(j, src_refs, ms):
        ss = [src_refs[g][...] for g in gs]
        m_news = [jnp.maximum(ms[g], jnp.max(ss[g], axis=0, keepdims=True)) for g in gs]
        alphas = [jnp.exp2(ms[g] - m_news[g]) for g in gs]
        ps = [jnp.exp2(ss[g] - m_news[g]).astype(BF16) for g in gs]
        pvs = [jnp.dot(v_aug(g, j), ps[g], preferred_element_type=F32) for g in gs]
        for g in gs:
            acc_refs[g][...] = alphas[g] * acc_refs[g][...] + pvs[g]
        return tuple(m_news)

    q_ts = [q_ref[:, cols(g)].astype(F32).T.astype(BF16) for g in gs]
    kms = [_split_bf16(kmean_ref[g, 0:n_gate, :]) for g in gs]
    gates = [jnp.dot(kms[g][0], q_ts[g], preferred_element_type=F32)
             + jnp.dot(kms[g][1], q_ts[g], preferred_element_type=F32) for g in gs]
    gates = [jnp.where(blk < i, gates[g], neg_inf) for g in gs]
    sels = [blk < 0 for g in gs]
    for _ in range(MOBA_TOP_K):
        tops = [jnp.max(gates[g], axis=0, keepdims=True) for g in gs]
        idxs = [jnp.min(jnp.where(gates[g] == tops[g], blk, LANES), axis=0, keepdims=True) for g in gs]
        hits = [blk == idxs[g] for g in gs]
        sels = [sels[g] | (hits[g] & (tops[g] > neg_inf)) for g in gs]
        gates = [jnp.where(hits[g], neg_inf, gates[g]) for g in gs]
    pad = jnp.full((LANES - n_gate, B), MASK_PENALTY, BF16)
    for g in gs:
        penalty = jnp.where(sels[g], 0.0, MASK_PENALTY).astype(BF16)
        parts = [q_ts[g], penalty] + ([pad] if n_gate < LANES else [])
        qa_refs[g][...] = jnp.concatenate(parts, axis=0)

    ss = [jnp.dot(k_ref[pl.ds(off_i, B), cols(g)], q_ts[g], preferred_element_type=F32) for g in gs]
    ss = [jnp.where(key_pos <= qry_pos, ss[g], neg_inf) for g in gs]
    m0s = [jnp.max(ss[g], axis=0, keepdims=True) for g in gs]
    ps = [jnp.exp2(ss[g] - m0s[g]).astype(BF16) for g in gs]
    for g in gs:
        acc_refs[g][...] = jnp.dot(v_aug(g, i), ps[g], preferred_element_type=F32)
    for g in gs:
        score(g, 0, sa_refs[g])

    def body(step, ms):
        j0 = ATTN_UNROLL * step
        bufs = (sa_refs, sb_refs)
        for t in range(ATTN_UNROLL):
            nxt = jnp.minimum(j0 + t + 1, nb - 1)
            for g in gs:
                score(g, nxt, bufs[(t + 1) % 2][g])
            ms = consume_all(jnp.minimum(j0 + t, nb - 1), bufs[t % 2], ms)
        return ms

    lax.fori_loop(0, (i + ATTN_UNROLL - 1) // ATTN_UNROLL, body, tuple(m0s))
    for g in gs:
        acc = acc_refs[g][...]
        out_t = acc[0:HEAD_DIM] / acc[HEAD_DIM:HEAD_DIM + 1]
        o_ref[:, cols(g)] = out_t.T.astype(o_ref.dtype)


def _moba_attention(qkv, v_t, n_heads, n_group):
    S = qkv.shape[0]
    assert S % MOBA_BLOCK == 0 and S // MOBA_BLOCK <= LANES and n_heads % n_group == 0
    B = MOBA_BLOCK
    nb = S // B
    gw = n_group * HEAD_DIM
    n_gate = min(LANES, -(-nb // BF16_ROWS) * BF16_ROWS)
    resident = pl.Buffered(1)
    scratch = ([pltpu.VMEM((HEAD_DIM + LANES, B), BF16)] * n_group
               + [pltpu.VMEM((B, B), F32)] * (2 * n_group)
               + [pltpu.VMEM((HEAD_DIM + ATTN_SUM_ROWS, B), F32)] * n_group)
    return pl.pallas_call(
        functools.partial(_attn_kernel, n_group=n_group, n_gate=n_gate),
        grid=(n_heads // n_group, nb),
        in_specs=[
            pl.BlockSpec((B, gw), lambda h, i: (i, h)),
            pl.BlockSpec((S, gw), lambda h, i: (0, n_heads // n_group + h), pipeline_mode=resident),
            pl.BlockSpec((nb, gw, B), lambda h, i: (0, h, 0), pipeline_mode=resident),
        ],
        out_specs=pl.BlockSpec((B, gw), lambda h, i: (i, h)),
        out_shape=jax.ShapeDtypeStruct((S, n_heads * HEAD_DIM), BF16),
        scratch_shapes=[pltpu.VMEM((n_group, LANES, HEAD_DIM), F32)] + scratch,
        compiler_params=_params(2),
        name="moba_attention",
    )(qkv, qkv, v_t)


def _outproj_kernel(a_ref, u_ref, w_ref, h_ref, g_ref, b_ref, of_ref, ob_ref, *, alpha):
    da = a_ref.shape[1]
    mix = (jnp.dot(a_ref[...], w_ref[0:da, :], preferred_element_type=F32)
           + jnp.dot(u_ref[...], w_ref[da:, :], preferred_element_type=F32))
    y = _layer_norm(alpha * h_ref[...] + mix, g_ref[...], b_ref[...])
    of_ref[...] = y
    ob_ref[...] = y.astype(BF16)


def _out_proj_ln(attn, u, w, h, g, b, alpha, tm):
    S, D = h.shape
    da, du = attn.shape[1], u.shape[1]
    return pl.pallas_call(
        functools.partial(_outproj_kernel, alpha=alpha),
        grid=(S // tm,),
        in_specs=[
            pl.BlockSpec((tm, da), lambda i: (i, 0)),
            pl.BlockSpec((tm, du), lambda i: (i, 0)),
            pl.BlockSpec((da + du, D), lambda i: (0, 0)),
            pl.BlockSpec((tm, D), lambda i: (i, 0)),
            pl.BlockSpec((1, D), lambda i: (0, 0)),
            pl.BlockSpec((1, D), lambda i: (0, 0)),
        ],
        out_specs=[pl.BlockSpec((tm, D), lambda i: (i, 0)), pl.BlockSpec((tm, D), lambda i: (i, 0))],
        out_shape=[jax.ShapeDtypeStruct((S, D), F32), jax.ShapeDtypeStruct((S, D), BF16)],
        compiler_params=_params(1),
        name="out_proj_ln",
    )(attn, u, w, h, g.reshape(1, D), b.reshape(1, D))


def _swiglu_step(x, wg_ref, wu_ref, wd_ref, acc_ref):
    gg = jnp.dot(x, wg_ref[...], preferred_element_type=F32)
    uu = jnp.dot(x, wu_ref[...], preferred_element_type=F32)
    a = (gg * _sigmoid(gg) * uu).astype(BF16)
    acc_ref[...] += jnp.dot(a, wd_ref[...], preferred_element_type=F32)


def _ffn_dense_kernel(x_ref, wg_ref, wu_ref, wd_ref, h_ref, g_ref, b_ref, *rest, alpha, n_cast):
    cast_in = rest[:n_cast]
    of_ref, ob_ref = rest[n_cast:n_cast + 2]
    cast_out = rest[n_cast + 2:2 * n_cast + 2]
    acc_ref = rest[2 * n_cast + 2]
    f = pl.program_id(1)

    @pl.when(f == 0)
    def _():
        acc_ref[...] = jnp.zeros_like(acc_ref)

    _swiglu_step(x_ref[...], wg_ref, wu_ref, wd_ref, acc_ref)

    for src, dst in zip(cast_in, cast_out):
        dst[...] = src[...].astype(dst.dtype)

    @pl.when(f == pl.num_programs(1) - 1)
    def _():
        y = _layer_norm(alpha * h_ref[...] + acc_ref[...], g_ref[...], b_ref[...])
        of_ref[...] = y
        ob_ref[...] = y.astype(BF16)


def _side_cast_spec(shape, ni, nf):
    E, rows, ncol = shape
    if ncol % (nf * LANES) == 0 and rows % (ni * BF16_ROWS) == 0:
        return pl.BlockSpec((E, rows // ni, ncol // nf), lambda i, f: (0, i, f))
    if rows % (ni * nf * BF16_ROWS) == 0:
        return pl.BlockSpec((E, rows // (ni * nf), ncol), lambda i, f: (0, i * nf + f, 0))
    return None


def _ffn_dense_ln(xb, wg, wu, wd, h, g, b, alpha, tm, tf, expert_w=()):
    S, D = h.shape
    dff = wg.shape[1]
    ni, nf = S // tm, dff // tf
    cast_specs = [_side_cast_spec(w.shape, ni, nf) for w in expert_w]
    n_cast = len(expert_w)
    outs = pl.pallas_call(
        functools.partial(_ffn_dense_kernel, alpha=alpha, n_cast=n_cast),
        grid=(ni, nf),
        in_specs=[
            pl.BlockSpec((tm, D), lambda i, f: (i, 0)),
            pl.BlockSpec((D, tf), lambda i, f: (0, f)),
            pl.BlockSpec((D, tf), lambda i, f: (0, f)),
            pl.BlockSpec((tf, D), lambda i, f: (f, 0)),
            pl.BlockSpec((tm, D), lambda i, f: (i, 0)),
            pl.BlockSpec((1, D), lambda i, f: (0, 0)),
            pl.BlockSpec((1, D), lambda i, f: (0, 0)),
        ] + cast_specs,
        out_specs=[pl.BlockSpec((tm, D), lambda i, f: (i, 0)), pl.BlockSpec((tm, D), lambda i, f: (i, 0))] + cast_specs,
        out_shape=[jax.ShapeDtypeStruct((S, D), F32), jax.ShapeDtypeStruct((S, D), BF16)]
        + [jax.ShapeDtypeStruct(w.shape, BF16) for w in expert_w],
        scratch_shapes=[pltpu.VMEM((tm, D), F32)],
        compiler_params=_params(2),
        name="ffn_dense_ln",
    )(xb, wg, wu, wd, h, g.reshape(1, D), b.reshape(1, D), *expert_w)
    return outs[0], outs[1], tuple(outs[2:])


def _ffn_grouped_kernel(te_ref, tv_ref, x_ref, wg_ref, wu_ref, wd_ref, y_ref, xb_ref, acc_ref):
    t = pl.program_id(0)
    f = pl.program_id(1)
    valid = tv_ref[t] == 1

    @pl.when(valid & (f == 0))
    def _():
        acc_ref[...] = jnp.zeros_like(acc_ref)
        xb_ref[...] = x_ref[...].astype(BF16)

    @pl.when(valid)
    def _():
        _swiglu_step(xb_ref[...], wg_ref, wu_ref, wd_ref, acc_ref)

    @pl.when(f == pl.num_programs(1) - 1)
    def _():
        y_ref[...] = jnp.where(valid, acc_ref[...], 0.0)


def _ffn_grouped(tile_expert, tile_valid, xperm, wg, wu, wd, tm, tf):
    R, D = xperm.shape
    dff = wg.shape[2]
    nf = dff // tf

    def chunk(t, f, tv):
        return jnp.where(tv[t] == 1, f, nf - 1)

    grid_spec = pltpu.PrefetchScalarGridSpec(
        num_scalar_prefetch=2,
        grid=(R // tm, nf),
        in_specs=[
            pl.BlockSpec((tm, D), lambda t, f, te, tv: (t, 0)),
            pl.BlockSpec((None, D, tf), lambda t, f, te, tv: (te[t], 0, chunk(t, f, tv))),
            pl.BlockSpec((None, D, tf), lambda t, f, te, tv: (te[t], 0, chunk(t, f, tv))),
            pl.BlockSpec((None, tf, D), lambda t, f, te, tv: (te[t], chunk(t, f, tv), 0)),
        ],
        out_specs=pl.BlockSpec((tm, D), lambda t, f, te, tv: (t, 0)),
        scratch_shapes=[pltpu.VMEM((tm, D), BF16), pltpu.VMEM((tm, D), F32)],
    )
    return pl.pallas_call(
        _ffn_grouped_kernel,
        grid_spec=grid_spec,
        out_shape=jax.ShapeDtypeStruct((R, D), F32),
        compiler_params=_params(2),
        name="ffn_grouped",
    )(tile_expert, tile_valid, xperm, wg, wu, wd)


def _router_kernel(h_ref, w_ref, meta_ref, cnt_ref, carry_ref, *, n_experts):
    i = pl.program_id(0)
    tr = h_ref.shape[0]

    @pl.when(i == 0)
    def _():
        carry_ref[...] = jnp.zeros_like(carry_ref)

    hh, hl = _split_bf16(h_ref[...])
    wh, wl = _split_bf16(w_ref[...])
    logits = (jnp.dot(hh, wh, preferred_element_type=F32) + jnp.dot(hh, wl, preferred_element_type=F32)
              + jnp.dot(hl, wh, preferred_element_type=F32))
    lane = lax.broadcasted_iota(jnp.int32, (tr, LANES), 1)
    neg_inf = -jnp.inf
    logits = jnp.where(lane < n_experts, logits, neg_inf)
    m0 = jnp.max(logits, axis=1, keepdims=True)
    i0 = jnp.min(jnp.where(logits == m0, lane, LANES), axis=1, keepdims=True)
    hit0 = lane == i0
    rest = jnp.where(hit0, neg_inf, logits)
    m1 = jnp.max(rest, axis=1, keepdims=True)
    i1 = jnp.min(jnp.where(rest == m1, lane, LANES), axis=1, keepdims=True)
    hit1 = lane == i1
    e = jnp.exp(m1 - m0)
    g0 = 1.0 / (1.0 + e)
    g1 = e / (1.0 + e)

    sel = jnp.where(hit0 | hit1, 1.0, 0.0)
    row = lax.broadcasted_iota(jnp.int32, (tr, tr), 0)
    col = lax.broadcasted_iota(jnp.int32, (tr, tr), 1)
    earlier = jnp.where(col < row, 1.0, 0.0).astype(BF16)
    before = jnp.dot(earlier, sel.astype(BF16), preferred_element_type=F32) + carry_ref[...]
    r0 = jnp.sum(jnp.where(hit0, before, 0.0), axis=1, keepdims=True)
    r1 = jnp.sum(jnp.where(hit1, before, 0.0), axis=1, keepdims=True)
    carry_ref[...] += jnp.sum(sel, axis=0, keepdims=True)

    cols = (i0.astype(F32), i1.astype(F32), g0, g1, r0, r1)
    meta = jnp.zeros((tr, LANES), F32)
    for c, val in enumerate(cols):
        meta = jnp.where(lane == c, val, meta)
    meta_ref[...] = meta
    cnt_ref[...] = jnp.broadcast_to(carry_ref[...], cnt_ref.shape)


def _router(h, w_pad, n_experts, tr):
    S, D = h.shape
    return pl.pallas_call(
        functools.partial(_router_kernel, n_experts=n_experts),
        grid=(S // tr,),
        in_specs=[pl.BlockSpec((tr, D), lambda i: (i, 0)), pl.BlockSpec((D, LANES), lambda i: (0, 0))],
        out_specs=[pl.BlockSpec((tr, LANES), lambda i: (i, 0)), pl.BlockSpec((8, LANES), lambda i: (0, 0))],
        out_shape=[jax.ShapeDtypeStruct((S, LANES), F32), jax.ShapeDtypeStruct((8, LANES), F32)],
        scratch_shapes=[pltpu.VMEM((1, LANES), F32)],
        compiler_params=_params(1),
        name="router",
    )(h, w_pad)


def _row_copy(src_ref, src_row, dst_ref, dst_row, sem):
    return pltpu.make_async_copy(src_ref.at[pl.ds(src_row, 1), :], dst_ref.at[pl.ds(dst_row, 1), :], sem)


def _dispatch_kernel(d0_ref, d1_ref, h_ref, xin_ref, xout_ref, buf_ref, sem):
    del xin_ref
    i = pl.program_id(0)
    n = pl.num_programs(0)
    ts = h_ref.shape[0]
    slot = i % 2

    def rows(step, s, act):
        base = step * ts

        def body(r, carry):
            act(_row_copy(buf_ref.at[s], r, xout_ref, d0_ref[base + r], sem.at[s]))
            act(_row_copy(buf_ref.at[s], r, xout_ref, d1_ref[base + r], sem.at[s]))
            return carry

        lax.fori_loop(0, ts, body, 0)

    @pl.when(i >= 2)
    def _():
        rows(i - 2, slot, lambda c: c.wait())

    buf_ref[slot] = h_ref[...]
    rows(i, slot, lambda c: c.start())

    @pl.when(i == n - 1)
    def _():
        @pl.when(i >= 1)
        def _():
            rows(i - 1, 1 - slot, lambda c: c.wait())

        rows(i, slot, lambda c: c.wait())


def _dispatch(d0, d1, h, n_rows, ts):
    S, D = h.shape
    grid_spec = pltpu.PrefetchScalarGridSpec(
        num_scalar_prefetch=2,
        grid=(S // ts,),
        in_specs=[pl.BlockSpec((ts, D), lambda i, a, b: (i, 0)), pl.BlockSpec(memory_space=pl.ANY)],
        out_specs=pl.BlockSpec(memory_space=pl.ANY),
        scratch_shapes=[pltpu.VMEM((2, ts, D), F32), pltpu.SemaphoreType.DMA((2,))],
    )
    return pl.pallas_call(
        _dispatch_kernel,
        grid_spec=grid_spec,
        out_shape=jax.ShapeDtypeStruct((n_rows, D), F32),
        input_output_aliases={3: 0},
        compiler_params=_params(1),
        name="moe_dispatch",
    )(d0, d1, h, jnp.zeros((n_rows, D), F32))


def _combine_kernel(d0_ref, d1_ref, meta_ref, h_ref, y_ref, g_ref, b_ref, of_ref, ob_ref,
                    y0_ref, y1_ref, sem, *, alpha):
    i = pl.program_id(0)
    n = pl.num_programs(0)
    ts = h_ref.shape[0]
    slot = i % 2

    def rows(step, s, act):
        base = step * ts

        def body(r, carry):
            act(_row_copy(y_ref, d0_ref[base + r], y0_ref.at[s], r, sem.at[s]))
            act(_row_copy(y_ref, d1_ref[base + r], y1_ref.at[s], r, sem.at[s]))
            return carry

        lax.fori_loop(0, ts, body, 0)

    @pl.when(i == 0)
    def _():
        rows(0, 0, lambda c: c.start())

    @pl.when(i + 1 < n)
    def _():
        rows(i + 1, 1 - slot, lambda c: c.start())

    rows(i, slot, lambda c: c.wait())
    meta = meta_ref[...]
    g0 = meta[:, 2:3]
    g1 = meta[:, 3:4]
    z = alpha * h_ref[...] + g0 * y0_ref[slot] + g1 * y1_ref[slot]
    y = _layer_norm(z, g_ref[...], b_ref[...])
    of_ref[...] = y
    ob_ref[...] = y.astype(BF16)


def _combine_ln(d0, d1, meta, h, y, g, b, alpha, ts):
    S, D = h.shape
    grid_spec = pltpu.PrefetchScalarGridSpec(
        num_scalar_prefetch=2,
        grid=(S // ts,),
        in_specs=[
            pl.BlockSpec((ts, LANES), lambda i, a, b: (i, 0)),
            pl.BlockSpec((ts, D), lambda i, a, b: (i, 0)),
            pl.BlockSpec(memory_space=pl.ANY),
            pl.BlockSpec((1, D), lambda i, a, b: (0, 0)),
            pl.BlockSpec((1, D), lambda i, a, b: (0, 0)),
        ],
        out_specs=[pl.BlockSpec((ts, D), lambda i, a, b: (i, 0)), pl.BlockSpec((ts, D), lambda i, a, b: (i, 0))],
        scratch_shapes=[pltpu.VMEM((2, ts, D), F32), pltpu.VMEM((2, ts, D), F32), pltpu.SemaphoreType.DMA((2,))],
    )
    return pl.pallas_call(
        functools.partial(_combine_kernel, alpha=alpha),
        grid_spec=grid_spec,
        out_shape=[jax.ShapeDtypeStruct((S, D), F32), jax.ShapeDtypeStruct((S, D), BF16)],
        compiler_params=_params(1),
        name="moe_combine_ln",
    )(d0, d1, meta, h, y, g.reshape(1, D), b.reshape(1, D))


def _tile(n, pref):
    t = min(n, pref)
    assert n % t == 0
    return t


def _ffn_chunk(dff, limit):
    return max(t for t in range(LANES, min(dff, limit) + 1, LANES) if dff % t == 0)


def _moe_layer(hf, hb, router_w, wg, wu, wd, ln_g, ln_b, alpha):
    S, D = hf.shape
    E = wg.shape[0]
    tm = _tile(S, 512)
    w_pad = jnp.zeros((D, LANES), F32).at[:, :E].set(router_w)
    meta, cnt = _router(hf, w_pad, E, _tile(S, 256))
    counts = cnt[0, :E].astype(jnp.int32)
    tiles_per = (counts + tm - 1) // tm
    tile_end = jnp.cumsum(tiles_per)
    start = (tile_end - tiles_per) * tm
    n_tiles = TOP_K_EXPERTS * S // tm + E
    tile_ids = jnp.arange(n_tiles, dtype=jnp.int32)
    tile_expert = jnp.minimum(jnp.sum(tile_ids[:, None] >= tile_end[None, :], axis=1), E - 1).astype(jnp.int32)
    tile_valid = (tile_ids < tile_end[E - 1]).astype(jnp.int32)
    e0 = meta[:, 0].astype(jnp.int32)
    e1 = meta[:, 1].astype(jnp.int32)
    d0 = start[e0] + meta[:, 4].astype(jnp.int32)
    d1 = start[e1] + meta[:, 5].astype(jnp.int32)
    ts = _tile(S, 256)
    xperm = _dispatch(d0, d1, hf, n_tiles * tm, ts)
    y = _ffn_grouped(tile_expert, tile_valid, xperm, wg, wu, wd, tm, _ffn_chunk(wg.shape[2], GROUPED_FF_CHUNK))
    return _combine_ln(d0, d1, meta, hf, y, ln_g, ln_b, alpha, ts)


def kernel(x, w_in, conv_dw_w, conv_dw_b, conv_ln_g, conv_ln_b, w_out, ln_mix_g, ln_mix_b, ln_ffn_g, ln_ffn_b,
           ffn_w_gate, ffn_w_up, ffn_w_down, router_w, expert_w_gate, expert_w_up, expert_w_down):
    batch, S, D = x.shape
    assert batch == 1
    depth = w_in.shape[0]
    conv_w = conv_dw_w.shape[2]
    attn_w = (w_in.shape[2] - 2 * conv_w) // 3
    n_heads = attn_w // HEAD_DIM
    alpha = (2.0 * depth) ** 0.25

    half = HEAD_DIM // 2
    inv_freq = jnp.power(ROPE_THETA, -jnp.arange(half, dtype=F32) / half)
    ang = jnp.arange(S, dtype=jnp.int32).astype(F32)[:, None] * inv_freq[None, :]
    cos = jnp.concatenate([jnp.cos(ang), jnp.cos(ang)], axis=-1)
    sin = jnp.concatenate([-jnp.sin(ang), jnp.sin(ang)], axis=-1)
    scale = HEAD_DIM ** -0.5 * LOG2_E
    cos_tab = jnp.stack([cos * scale, cos])
    sin_tab = jnp.stack([sin * scale, sin])

    hf = x[0]
    hb = hf.astype(BF16)
    tm_proj = _tile(S, 1024)
    for l in range(depth):
        w_in_b = w_in[l].astype(BF16)
        qkv = _qkv_proj(hb, w_in_b, cos_tab, sin_tab, attn_w, tm_proj, _tile(attn_w, 1024))
        u = _glu_proj(hb, w_in_b, attn_w, conv_w, tm_proj, _tile(conv_w, 1024))
        u = _conv_module(u, conv_dw_w[l], conv_dw_b[l], conv_ln_g[l], conv_ln_b[l], _tile(S, 512))
        v_t = qkv[:, 2 * attn_w:].reshape(S // MOBA_BLOCK, MOBA_BLOCK, attn_w).transpose(0, 2, 1)
        attn = _moba_attention(qkv, v_t, n_heads, min(n_heads, ATTN_HEAD_GROUP))
        hf, hb = _out_proj_ln(attn, u, w_out[l].astype(BF16), hf, ln_mix_g[l], ln_mix_b[l], alpha, _tile(S, 512))
        j = l // 2
        if l % 2 == 0:
            tm, tf = _tile(S, 512), _tile(ffn_w_gate.shape[2], 512)
            side = (expert_w_gate[j], expert_w_up[j], expert_w_down[j]) if l + 1 < depth else ()
            if any(_side_cast_spec(w.shape, S // tm, ffn_w_gate.shape[2] // tf) is None for w in side):
                side = ()
            hf, hb, experts_b = _ffn_dense_ln(hb, ffn_w_gate[j].astype(BF16), ffn_w_up[j].astype(BF16),
                                              ffn_w_down[j].astype(BF16), hf, ln_ffn_g[l], ln_ffn_b[l], alpha,
                                              tm, tf, side)
        else:
            if not experts_b:
                experts_b = (expert_w_gate[j].astype(BF16), expert_w_up[j].astype(BF16), expert_w_down[j].astype(BF16))
            hf, hb = _moe_layer(hf, hb, router_w[j], *experts_b, ln_ffn_g[l], ln_ffn_b[l], alpha)
    return hf[None]
```

```python
import functools

import jax
import jax.numpy as jnp
from jax import lax
from jax.experimental import pallas as pl
from jax.experimental.pallas import tpu as pltpu

F32 = jnp.float32
BF16 = jnp.bfloat16

HEAD_DIM = 128
MOBA_BLOCK = 256
MOBA_TOP_K = 3
CONV_KERNEL = 31
ROPE_THETA = 10000.0
LN_EPS = 1e-5
TOP_K_EXPERTS = 2

LANES = 128
SUBLANES = 8
CONV_HALO = 32
CONV_ROWS = 32
MASK_PENALTY = -1e30
LOG2_E = 1.4426950408889634
ATTN_HEAD_GROUP = 4
GROUPED_FF_CHUNK = 512
ROW_DMA_UNROLL = 8
ATTN_UNROLL = 4
BF16_ROWS = 16
ATTN_SUM_ROWS = BF16_ROWS
VMEM_LIMIT = 56 * 1024 * 1024

_DN_T = (((1,), (1,)), ((), ()))


def _params(n_axes):
    return pltpu.CompilerParams(dimension_semantics=("arbitrary",) * n_axes,
                                vmem_limit_bytes=VMEM_LIMIT)


def _sigmoid(x):
    return 1.0 / (1.0 + jnp.exp(-x))


def _layer_norm(z, g, b):
    mu = jnp.mean(z, axis=-1, keepdims=True)
    zc = z - mu
    var = jnp.mean(zc * zc, axis=-1, keepdims=True)
    return zc * lax.rsqrt(var + LN_EPS) * g + b


def _split_bf16(x):
    hi = x.astype(BF16)
    lo = (x - hi.astype(F32)).astype(BF16)
    return hi, lo


def _qkv_kernel(x_ref, w_ref, cos_ref, sin_ref, o_ref, *, n_rot_blocks):
    j = pl.program_id(1)
    acc = jnp.dot(x_ref[...], w_ref[...], preferred_element_type=F32)

    @pl.when(j < n_rot_blocks)
    def _():
        cos = cos_ref[0]
        sin = sin_ref[0]
        for c in range(0, acc.shape[1], HEAD_DIM):
            a = acc[:, c:c + HEAD_DIM]
            r = a * cos + pltpu.roll(a, HEAD_DIM // 2, 1) * sin
            o_ref[:, c:c + HEAD_DIM] = r.astype(o_ref.dtype)

    @pl.when(j >= n_rot_blocks)
    def _():
        o_ref[...] = acc.astype(o_ref.dtype)


def _qkv_proj(xb, w, cos_tab, sin_tab, attn_w, tm, tn):
    S, D = xb.shape
    n_sec = attn_w // tn
    return pl.pallas_call(
        functools.partial(_qkv_kernel, n_rot_blocks=2 * n_sec),
        grid=(S // tm, 3 * n_sec),
        in_specs=[
            pl.BlockSpec((tm, D), lambda i, j: (i, 0)),
            pl.BlockSpec((D, tn), lambda i, j: (0, j)),
            pl.BlockSpec((1, tm, HEAD_DIM), lambda i, j: (jnp.minimum(j // n_sec, 1), i, 0)),
            pl.BlockSpec((1, tm, HEAD_DIM), lambda i, j: (jnp.minimum(j // n_sec, 1), i, 0)),
        ],
        out_specs=pl.BlockSpec((tm, tn), lambda i, j: (i, j)),
        out_shape=jax.ShapeDtypeStruct((S, 3 * attn_w), BF16),
        compiler_params=_params(2),
        name="qkv_proj",
    )(xb, w, cos_tab, sin_tab)


def _glu_kernel(x_ref, wa_ref, wg_ref, o_ref):
    x = x_ref[...]
    a = jnp.dot(x, wa_ref[...], preferred_element_type=F32)
    g = jnp.dot(x, wg_ref[...], preferred_element_type=F32)
    o_ref[...] = a * _sigmoid(g)


def _glu_proj(xb, w, attn_w, conv_w, tm, tn):
    S, D = xb.shape
    a0 = 3 * attn_w // tn
    g0 = (3 * attn_w + conv_w) // tn
    return pl.pallas_call(
        _glu_kernel,
        grid=(S // tm, conv_w // tn),
        in_specs=[
            pl.BlockSpec((tm, D), lambda i, j: (i, 0)),
            pl.BlockSpec((D, tn), lambda i, j: (0, a0 + j)),
            pl.BlockSpec((D, tn), lambda i, j: (0, g0 + j)),
        ],
        out_specs=pl.BlockSpec((tm, tn), lambda i, j: (i, j)),
        out_shape=jax.ShapeDtypeStruct((S, conv_w), F32),
        compiler_params=_params(2),
        name="glu_proj",
    )(xb, w, w)


def _conv_kernel(u_ref, halo_ref, w_ref, b_ref, g_ref, beta_ref, o_ref, ext_ref, y_ref):
    i = pl.program_id(0)
    tc = u_ref.shape[0]
    ext_ref[0:CONV_HALO, :] = jnp.where(i > 0, halo_ref[...], 0.0)
    ext_ref[CONV_HALO:CONV_HALO + tc, :] = u_ref[...]
    bias = b_ref[...]
    g = g_ref[...]
    beta = beta_ref[...]
    first = CONV_HALO - (CONV_KERNEL - 1)
    n_acc = CONV_ROWS // SUBLANES
    sub = lax.broadcasted_iota(jnp.int32, (SUBLANES, LANES), 0)

    def body(r, carry):
        base = pl.multiple_of(r * CONV_ROWS, CONV_ROWS)
        for c in range(0, u_ref.shape[1], LANES):
            win = ext_ref[pl.ds(base, 2 * CONV_ROWS), c:c + LANES]
            groups = [win[v:v + SUBLANES] for v in range(0, 2 * CONV_ROWS, SUBLANES)]
            accs = [jnp.zeros((SUBLANES, LANES), F32) + bias[:, c:c + LANES]] * n_acc
            for shift in range(SUBLANES):
                taps = [k for k in range(CONV_KERNEL) if (first + k) % SUBLANES == shift]
                if shift == 0:
                    shifted = groups
                else:
                    shifted = [pltpu.roll(jnp.where(sub >= shift, groups[v], groups[v + 1]), SUBLANES - shift, 0)
                               for v in range(len(groups) - 1)]
                for k in taps:
                    a = (first + k) // SUBLANES
                    wk = w_ref[k:k + 1, c:c + LANES]
                    accs = [accs[v] + wk * shifted[a + v] for v in range(n_acc)]
            y_ref[pl.ds(base, CONV_ROWS), c:c + LANES] = jnp.concatenate(accs, axis=0)
        y = _layer_norm(y_ref[pl.ds(base, CONV_ROWS), :], g, beta)
        o_ref[pl.ds(base, CONV_ROWS), :] = (y * _sigmoid(y)).astype(o_ref.dtype)
        return carry

    lax.fori_loop(0, tc // CONV_ROWS, body, 0)


def _conv_module(u, w, b, g, beta, tc):
    S, C = u.shape
    per = tc // CONV_HALO
    return pl.pallas_call(
        _conv_kernel,
        grid=(S // tc,),
        in_specs=[
            pl.BlockSpec((tc, C), lambda i: (i, 0)),
            pl.BlockSpec((CONV_HALO, C), lambda i: (jnp.maximum(i * per - 1, 0), 0)),
            pl.BlockSpec((CONV_KERNEL, C), lambda i: (0, 0)),
            pl.BlockSpec((1, C), lambda i: (0, 0)),
            pl.BlockSpec((1, C), lambda i: (0, 0)),
            pl.BlockSpec((1, C), lambda i: (0, 0)),
        ],
        out_specs=pl.BlockSpec((tc, C), lambda i: (i, 0)),
        out_shape=jax.ShapeDtypeStruct((S, C), BF16),
        scratch_shapes=[pltpu.VMEM((CONV_HALO + tc, C), F32), pltpu.VMEM((tc, C), F32)],
        compiler_params=_params(1),
        name="conv_module",
    )(u, u, w, b.reshape(1, C), g.reshape(1, C), beta.reshape(1, C))


def _attn_kernel(q_ref, k_ref, vt_ref, o_ref, kmean_ref, *scratch, n_group, n_gate):
    qa_refs = scratch[0:n_group]
    sa_refs = scratch[n_group:2 * n_group]
    sb_refs = scratch[2 * n_group:3 * n_group]
    acc_refs = scratch[3 * n_group:4 * n_group]
    i = pl.program_id(1)
    B = MOBA_BLOCK
    nb = k_ref.shape[0] // B
    gs = range(n_group)
    lane = lax.broadcasted_iota(jnp.int32, (B, LANES), 1)
    blk = lax.broadcasted_iota(jnp.int32, (n_gate, B), 0)
    ones_rows = jnp.ones((ATTN_SUM_ROWS, B), BF16)

    @pl.when(i == 0)
    def _():
        kmean_ref[...] = jnp.zeros_like(kmean_ref)

        def fill(j, carry):
            off = pl.multiple_of(j * B, B)
            for g in gs:
                kb = k_ref[pl.ds(off, B), g * HEAD_DIM:(g + 1) * HEAD_DIM]
                kmean_ref[g, pl.ds(j, 1), :] = jnp.sum(kb.astype(F32), axis=0, keepdims=True) * (1.0 / B)
            return carry

        lax.fori_loop(0, nb, fill, 0)

    neg_inf = -jnp.inf
    key_pos = lax.broadcasted_iota(jnp.int32, (B, B), 0)
    qry_pos = lax.broadcasted_iota(jnp.int32, (B, B), 1)
    off_i = pl.multiple_of(i * B, B)

    def cols(g):
        return slice(g * HEAD_DIM, (g + 1) * HEAD_DIM)

    def v_aug(g, j):
        return jnp.concatenate([vt_ref[j, cols(g), :], ones_rows], axis=0)

    def score(g, j, dst_ref):
        off = pl.multiple_of(j * B, B)
        onehot = jnp.where(lane == j, 1.0, 0.0).astype(BF16)
        k_cat = jnp.concatenate([k_ref[pl.ds(off, B), cols(g)], onehot], axis=1)
        dst_ref[...] = jnp.dot(k_cat, qa_refs[g][...], preferred_element_type=F32)

    def consume_all(j, src_refs, ms):
        ss = [src_refs[g][...] for g in gs]
        m_news = [jnp.maximum(ms[g], jnp.max(ss[g], axis=0, keepdims=True)) for g in gs]
        alphas = [jnp.exp2(ms[g] - m_news[g]) for g in gs]
        ps = [jnp.exp2(ss[g] - m_news[g]).astype(BF16) for g in gs]
        pvs = [jnp.dot(v_aug(g, j), ps[g], preferred_element_type=F32) for g in gs]
        for g in gs:
            acc_refs[g][...] = alphas[g] * acc_refs[g][...] + pvs[g]
        return tuple(m_news)

    q_ts = [q_ref[:, cols(g)].astype(F32).T.astype(BF16) for g in gs]
    kms = [_split_bf16(kmean_ref[g, 0:n_gate, :]) for g in gs]
    gates = [jnp.dot(kms[g][0], q_ts[g], preferred_element_type=F32)
             + jnp.dot(kms[g][1], q_ts[g], preferred_element_type=F32) for g in gs]
    gates = [jnp.where(blk < i, gates[g], neg_inf) for g in gs]
    sels = [blk < 0 for g in gs]
    for _ in range(MOBA_TOP_K):
        tops = [jnp.max(gates[g], axis=0, keepdims=True) for g in gs]
        idxs = [jnp.min(jnp.where(gates[g] == tops[g], blk, LANES), axis=0, keepdims=True) for g in gs]
        hits = [blk == idxs[g] for g in gs]
        sels = [sels[g] | (hits[g] & (tops[g] > neg_inf)) for g in gs]
        gates = [jnp.where(hits[g], neg_inf, gates[g]) for g in gs]
    pad = jnp.full((LANES - n_gate, B), MASK_PENALTY, BF16)
    for g in gs:
        penalty = jnp.where(sels[g], 0.0, MASK_PENALTY).astype(BF16)
        parts = [q_ts[g], penalty] + ([pad] if n_gate < LANES else [])
        qa_refs[g][...] = jnp.concatenate(parts, axis=0)

    ss = [jnp.dot(k_ref[pl.ds(off_i, B), cols(g)], q_ts[g], preferred_element_type=F32) for g in gs]
    ss = [jnp.where(key_pos <= qry_pos, ss[g], neg_inf) for g in gs]
    m0s = [jnp.max(ss[g], axis=0, keepdims=True) for g in gs]
    ps = [jnp.exp2(ss[g] - m0s[g]).astype(BF16) for g in gs]
    for g in gs:
        acc_refs[g][...] = jnp.dot(v_aug(g, i), ps[g], preferred_element_type=F32)
    for g in gs:
        score(g, 0, sa_refs[g])

    def body(step, ms):
        j0 = ATTN_UNROLL * step
        bufs = (sa_refs, sb_refs)
        for t in range(ATTN_UNROLL):
            nxt = jnp.minimum(j0 + t + 1, nb - 1)
            for g in gs:
                score(g, nxt, bufs[(t + 1) % 2][g])
            ms = consume_all(jnp.minimum(j0 + t, nb - 1), bufs[t % 2], ms)
        return ms

    lax.fori_loop(0, (i + ATTN_UNROLL - 1) // ATTN_UNROLL, body, tuple(m0s))
    for g in gs:
        acc = acc_refs[g][...]
        out_t = acc[0:HEAD_DIM] / acc[HEAD_DIM:HEAD_DIM + 1]
        o_ref[:, cols(g)] = out_t.T.astype(o_ref.dtype)


def _moba_attention(qkv, v_t, n_heads, n_group):
    S = qkv.shape[0]
    assert S % MOBA_BLOCK == 0 and S // MOBA_BLOCK <= LANES and n_heads % n_group == 0
    B = MOBA_BLOCK
    nb = S // B
    gw = n_group * HEAD_DIM
    n_gate = min(LANES, -(-nb // BF16_ROWS) * BF16_ROWS)
    resident = pl.Buffered(1)
    scratch = ([pltpu.VMEM((HEAD_DIM + LANES, B), BF16)] * n_group
               + [pltpu.VMEM((B, B), F32)] * (2 * n_group)
               + [pltpu.VMEM((HEAD_DIM + ATTN_SUM_ROWS, B), F32)] * n_group)
    return pl.pallas_call(
        functools.partial(_attn_kernel, n_group=n_group, n_gate=n_gate),
        grid=(n_heads // n_group, nb),
        in_specs=[
            pl.BlockSpec((B, gw), lambda h, i: (i, h)),
            pl.BlockSpec((S, gw), lambda h, i: (0, n_heads // n_group + h), pipeline_mode=resident),
            pl.BlockSpec((nb, gw, B), lambda h, i: (0, h, 0), pipeline_mode=resident),
        ],
        out_specs=pl.BlockSpec((B, gw), lambda h, i: (i, h)),
        out_shape=jax.ShapeDtypeStruct((S, n_heads * HEAD_DIM), BF16),
        scratch_shapes=[pltpu.VMEM((n_group, LANES, HEAD_DIM), F32)] + scratch,
        compiler_params=_params(2),
        name="moba_attention",
    )(qkv, qkv, v_t)


def _outproj_kernel(a_ref, u_ref, w_ref, h_ref, g_ref, b_ref, of_ref, ob_ref, *, alpha):
    da = a_ref.shape[1]
    mix = (jnp.dot(a_ref[...], w_ref[0:da, :], preferred_element_type=F32)
           + jnp.dot(u_ref[...], w_ref[da:, :], preferred_element_type=F32))
    y = _layer_norm(alpha * h_ref[...] + mix, g_ref[...], b_ref[...])
    of_ref[...] = y
    ob_ref[...] = y.astype(BF16)


def _out_proj_ln(attn, u, w, h, g, b, alpha, tm):
    S, D = h.shape
    da, du = attn.shape[1], u.shape[1]
    return pl.pallas_call(
        functools.partial(_outproj_kernel, alpha=alpha),
        grid=(S // tm,),
        in_specs=[
            pl.BlockSpec((tm, da), lambda i: (i, 0)),
            pl.BlockSpec((tm, du), lambda i: (i, 0)),
            pl.BlockSpec((da + du, D), lambda i: (0, 0)),
            pl.BlockSpec((tm, D), lambda i: (i, 0)),
            pl.BlockSpec((1, D), lambda i: (0, 0)),
            pl.BlockSpec((1, D), lambda i: (0, 0)),
        ],
        out_specs=[pl.BlockSpec((tm, D), lambda i: (i, 0)), pl.BlockSpec((tm, D), lambda i: (i, 0))],
        out_shape=[jax.ShapeDtypeStruct((S, D), F32), jax.ShapeDtypeStruct((S, D), BF16)],
        compiler_params=_params(1),
        name="out_proj_ln",
    )(attn, u, w, h, g.reshape(1, D), b.reshape(1, D))


def _swiglu_step(x, wg_ref, wu_ref, wd_ref, acc_ref):
    gg = jnp.dot(x, wg_ref[...], preferred_element_type=F32)
    uu = jnp.dot(x, wu_ref[...], preferred_element_type=F32)
    a = (gg * _sigmoid(gg) * uu).astype(BF16)
    acc_ref[...] += jnp.dot(a, wd_ref[...], preferred_element_type=F32)


def _ffn_dense_kernel(x_ref, wg_ref, wu_ref, wd_ref, h_ref, g_ref, b_ref, *rest, alpha, n_cast):
    cast_in = rest[:n_cast]
    of_ref, ob_ref = rest[n_cast:n_cast + 2]
    cast_out = rest[n_cast + 2:2 * n_cast + 2]
    acc_ref = rest[2 * n_cast + 2]
    f = pl.program_id(1)

    @pl.when(f == 0)
    def _():
        acc_ref[...] = jnp.zeros_like(acc_ref)

    _swiglu_step(x_ref[...], wg_ref, wu_ref, wd_ref, acc_ref)

    for src, dst in zip(cast_in, cast_out):
        dst[...] = src[...].astype(dst.dtype)

    @pl.when(f == pl.num_programs(1) - 1)
    def _():
        y = _layer_norm(alpha * h_ref[...] + acc_ref[...], g_ref[...], b_ref[...])
        of_ref[...] = y
        ob_ref[...] = y.astype(BF16)


def _side_cast_spec(shape, ni, nf):
    E, rows, ncol = shape
    if ncol % (nf * LANES) == 0 and rows % (ni * BF16_ROWS) == 0:
        return pl.BlockSpec((E, rows // ni, ncol // nf), lambda i, f: (0, i, f))
    if rows % (ni * nf * BF16_ROWS) == 0:
        return pl.BlockSpec((E, rows // (ni * nf), ncol), lambda i, f: (0, i * nf + f, 0))
    return None


def _ffn_dense_ln(xb, wg, wu, wd, h, g, b, alpha, tm, tf, expert_w=()):
    S, D = h.shape
    dff = wg.shape[1]
    ni, nf = S // tm, dff // tf
    cast_specs = [_side_cast_spec(w.shape, ni, nf) for w in expert_w]
    n_cast = len(expert_w)
    outs = pl.pallas_call(
        functools.partial(_ffn_dense_kernel, alpha=alpha, n_cast=n_cast),
        grid=(ni, nf),
        in_specs=[
            pl.BlockSpec((tm, D), lambda i, f: (i, 0)),
            pl.BlockSpec((D, tf), lambda i, f: (0, f)),
            pl.BlockSpec((D, tf), lambda i, f: (0, f)),
            pl.BlockSpec((tf, D), lambda i, f: (f, 0)),
            pl.BlockSpec((tm, D), lambda i, f: (i, 0)),
            pl.BlockSpec((1, D), lambda i, f: (0, 0)),
            pl.BlockSpec((1, D), lambda i, f: (0, 0)),
        ] + cast_specs,
        out_specs=[pl.BlockSpec((tm, D), lambda i, f: (i, 0)), pl.BlockSpec((tm, D), lambda i, f: (i, 0))] + cast_specs,
        out_shape=[jax.ShapeDtypeStruct((S, D), F32), jax.ShapeDtypeStruct((S, D), BF16)]
        + [jax.ShapeDtypeStruct(w.shape, BF16) for w in expert_w],
        scratch_shapes=[pltpu.VMEM((tm, D), F32)],
        compiler_params=_params(2),
        name="ffn_dense_ln",
    )(xb, wg, wu, wd, h, g.reshape(1, D), b.reshape(1, D), *expert_w)
    return outs[0], outs[1], tuple(outs[2:])


def _ffn_grouped_kernel(te_ref, tv_ref, x_ref, wg_ref, wu_ref, wd_ref, y_ref, xb_ref, acc_ref):
    t = pl.program_id(0)
    f = pl.program_id(1)
    valid = tv_ref[t] == 1

    @pl.when(valid & (f == 0))
    def _():
        acc_ref[...] = jnp.zeros_like(acc_ref)
        xb_ref[...] = x_ref[...].astype(BF16)

    @pl.when(valid)
    def _():
        _swiglu_step(xb_ref[...], wg_ref, wu_ref, wd_ref, acc_ref)

    @pl.when(f == pl.num_programs(1) - 1)
    def _():
        y_ref[...] = jnp.where(valid, acc_ref[...], 0.0)


def _ffn_grouped(tile_expert, tile_valid, xperm, wg, wu, wd, tm, tf):
    R, D = xperm.shape
    dff = wg.shape[2]
    nf = dff // tf

    def chunk(t, f, tv):
        return jnp.where(tv[t] == 1, f, nf - 1)

    grid_spec = pltpu.PrefetchScalarGridSpec(
        num_scalar_prefetch=2,
        grid=(R // tm, nf),
        in_specs=[
            pl.BlockSpec((tm, D), lambda t, f, te, tv: (t, 0)),
            pl.BlockSpec((None, D, tf), lambda t, f, te, tv: (te[t], 0, chunk(t, f, tv))),
            pl.BlockSpec((None, D, tf), lambda t, f, te, tv: (te[t], 0, chunk(t, f, tv))),
            pl.BlockSpec((None, tf, D), lambda t, f, te, tv: (te[t], chunk(t, f, tv), 0)),
        ],
        out_specs=pl.BlockSpec((tm, D), lambda t, f, te, tv: (t, 0)),
        scratch_shapes=[pltpu.VMEM((tm, D), BF16), pltpu.VMEM((tm, D), F32)],
    )
    return pl.pallas_call(
        _ffn_grouped_kernel,
        grid_spec=grid_spec,
        out_shape=jax.ShapeDtypeStruct((R, D), F32),
        compiler_params=_params(2),
        name="ffn_grouped",
    )(tile_expert, tile_valid, xperm, wg, wu, wd)


def _router_kernel(h_ref, w_ref, meta_ref, cnt_ref, carry_ref, *, n_experts):
    i = pl.program_id(0)
    tr = h_ref.shape[0]

    @pl.when(i == 0)
    def _():
        carry_ref[...] = jnp.zeros_like(carry_ref)

    hh, hl = _split_bf16(h_ref[...])
    wh, wl = _split_bf16(w_ref[...])
    logits = (jnp.dot(hh, wh, preferred_element_type=F32) + jnp.dot(hh, wl, preferred_element_type=F32)
              + jnp.dot(hl, wh, preferred_element_type=F32))
    lane = lax.broadcasted_iota(jnp.int32, (tr, LANES), 1)
    neg_inf = -jnp.inf
    logits = jnp.where(lane < n_experts, logits, neg_inf)
    m0 = jnp.max(logits, axis=1, keepdims=True)
    i0 = jnp.min(jnp.where(logits == m0, lane, LANES), axis=1, keepdims=True)
    hit0 = lane == i0
    rest = jnp.where(hit0, neg_inf, logits)
    m1 = jnp.max(rest, axis=1, keepdims=True)
    i1 = jnp.min(jnp.where(rest == m1, lane, LANES), axis=1, keepdims=True)
    hit1 = lane == i1
    e = jnp.exp(m1 - m0)
    g0 = 1.0 / (1.0 + e)
    g1 = e / (1.0 + e)

    sel = jnp.where(hit0 | hit1, 1.0, 0.0)
    row = lax.broadcasted_iota(jnp.int32, (tr, tr), 0)
    col = lax.broadcasted_iota(jnp.int32, (tr, tr), 1)
    earlier = jnp.where(col < row, 1.0, 0.0).astype(BF16)
    before = jnp.dot(earlier, sel.astype(BF16), preferred_element_type=F32) + carry_ref[...]
    r0 = jnp.sum(jnp.where(hit0, before, 0.0), axis=1, keepdims=True)
    r1 = jnp.sum(jnp.where(hit1, before, 0.0), axis=1, keepdims=True)
    carry_ref[...] += jnp.sum(sel, axis=0, keepdims=True)

    cols = (i0.astype(F32), i1.astype(F32), g0, g1, r0, r1)
    meta = jnp.zeros((tr, LANES), F32)
    for c, val in enumerate(cols):
        meta = jnp.where(lane == c, val, meta)
    meta_ref[...] = meta
    cnt_ref[...] = jnp.broadcast_to(carry_ref[...], cnt_ref.shape)


def _router(h, w_pad, n_experts, tr):
    S, D = h.shape
    return pl.pallas_call(
        functools.partial(_router_kernel, n_experts=n_experts),
        grid=(S // tr,),
        in_specs=[pl.BlockSpec((tr, D), lambda i: (i, 0)), pl.BlockSpec((D, LANES), lambda i: (0, 0))],
        out_specs=[pl.BlockSpec((tr, LANES), lambda i: (i, 0)), pl.BlockSpec((8, LANES), lambda i: (0, 0))],
        out_shape=[jax.ShapeDtypeStruct((S, LANES), F32), jax.ShapeDtypeStruct((8, LANES), F32)],
        scratch_shapes=[pltpu.VMEM((1, LANES), F32)],
        compiler_params=_params(1),
        name="router",
    )(h, w_pad)


def _row_copy(src_ref, src_row, dst_ref, dst_row, sem):
    return pltpu.make_async_copy(src_ref.at[pl.ds(src_row, 1), :], dst_ref.at[pl.ds(dst_row, 1), :], sem)


def _dispatch_kernel(d0_ref, d1_ref, h_ref, xin_ref, xout_ref, buf_ref, sem):
    del xin_ref
    i = pl.program_id(0)
    n = pl.num_programs(0)
    ts = h_ref.shape[0]
    slot = i % 2

    def rows(step, s, act):
        base = step * ts

        def body(r, carry):
            act(_row_copy(buf_ref.at[s], r, xout_ref, d0_ref[base + r], sem.at[s]))
            act(_row_copy(buf_ref.at[s], r, xout_ref, d1_ref[base + r], sem.at[s]))
            return carry

        lax.fori_loop(0, ts, body, 0, unroll=ROW_DMA_UNROLL)

    @pl.when(i >= 2)
    def _():
        rows(i - 2, slot, lambda c: c.wait())

    buf_ref[slot] = h_ref[...]
    rows(i, slot, lambda c: c.start())

    @pl.when(i == n - 1)
    def _():
        @pl.when(i >= 1)
        def _():
            rows(i - 1, 1 - slot, lambda c: c.wait())

        rows(i, slot, lambda c: c.wait())


def _dispatch(d0, d1, h, n_rows, ts):
    S, D = h.shape
    grid_spec = pltpu.PrefetchScalarGridSpec(
        num_scalar_prefetch=2,
        grid=(S // ts,),
        in_specs=[pl.BlockSpec((ts, D), lambda i, a, b: (i, 0)), pl.BlockSpec(memory_space=pl.ANY)],
        out_specs=pl.BlockSpec(memory_space=pl.ANY),
        scratch_shapes=[pltpu.VMEM((2, ts, D), F32), pltpu.SemaphoreType.DMA((2,))],
    )
    return pl.pallas_call(
        _dispatch_kernel,
        grid_spec=grid_spec,
        out_shape=jax.ShapeDtypeStruct((n_rows, D), F32),
        input_output_aliases={3: 0},
        compiler_params=_params(1),
        name="moe_dispatch",
    )(d0, d1, h, jnp.zeros((n_rows, D), F32))


def _combine_kernel(d0_ref, d1_ref, meta_ref, h_ref, y_ref, g_ref, b_ref, *rest, alpha):
    out_refs = rest[:-3]
    y0_ref, y1_ref, sem = rest[-3:]
    i = pl.program_id(0)
    n = pl.num_programs(0)
    ts = h_ref.shape[0]
    slot = i % 2

    def rows(step, s, act):
        base = step * ts

        def body(r, carry):
            act(_row_copy(y_ref, d0_ref[base + r], y0_ref.at[s], r, sem.at[s]))
            act(_row_copy(y_ref, d1_ref[base + r], y1_ref.at[s], r, sem.at[s]))
            return carry

        lax.fori_loop(0, ts, body, 0, unroll=ROW_DMA_UNROLL)

    @pl.when(i == 0)
    def _():
        rows(0, 0, lambda c: c.start())

    @pl.when(i + 1 < n)
    def _():
        rows(i + 1, 1 - slot, lambda c: c.start())

    rows(i, slot, lambda c: c.wait())
    meta = meta_ref[...]
    g0 = meta[:, 2:3]
    g1 = meta[:, 3:4]
    z = alpha * h_ref[...] + g0 * y0_ref[slot] + g1 * y1_ref[slot]
    y = _layer_norm(z, g_ref[...], b_ref[...])
    for o_ref in out_refs:
        o_ref[...] = y.astype(o_ref.dtype)


def _combine_ln(d0, d1, meta, h, y, g, b, alpha, ts, want_bf16):
    S, D = h.shape
    out_dtypes = [F32, BF16] if want_bf16 else [F32]
    grid_spec = pltpu.PrefetchScalarGridSpec(
        num_scalar_prefetch=2,
        grid=(S // ts,),
        in_specs=[
            pl.BlockSpec((ts, LANES), lambda i, a, b: (i, 0)),
            pl.BlockSpec((ts, D), lambda i, a, b: (i, 0)),
            pl.BlockSpec(memory_space=pl.ANY),
            pl.BlockSpec((1, D), lambda i, a, b: (0, 0)),
            pl.BlockSpec((1, D), lambda i, a, b: (0, 0)),
        ],
        out_specs=[pl.BlockSpec((ts, D), lambda i, a, b: (i, 0)) for _ in out_dtypes],
        scratch_shapes=[pltpu.VMEM((2, ts, D), F32), pltpu.VMEM((2, ts, D), F32), pltpu.SemaphoreType.DMA((2,))],
    )
    outs = pl.pallas_call(
        functools.partial(_combine_kernel, alpha=alpha),
        grid_spec=grid_spec,
        out_shape=[jax.ShapeDtypeStruct((S, D), dt) for dt in out_dtypes],
        compiler_params=_params(1),
        name="moe_combine_ln",
    )(d0, d1, meta, h, y, g.reshape(1, D), b.reshape(1, D))
    return outs[0], (outs[1] if want_bf16 else None)


def _tile(n, pref):
    t = min(n, pref)
    assert n % t == 0
    return t


def _ffn_chunk(dff, limit):
    return max(t for t in range(LANES, min(dff, limit) + 1, LANES) if dff % t == 0)


def _moe_layer(hf, router_w, wg, wu, wd, ln_g, ln_b, alpha, want_bf16):
    S, D = hf.shape
    E = wg.shape[0]
    tm = _tile(S, 512)
    w_pad = jnp.zeros((D, LANES), F32).at[:, :E].set(router_w)
    meta, cnt = _router(hf, w_pad, E, _tile(S, 256))
    counts = cnt[0, :E].astype(jnp.int32)
    tiles_per = (counts + tm - 1) // tm
    tile_end = jnp.cumsum(tiles_per)
    start = (tile_end - tiles_per) * tm
    n_tiles = TOP_K_EXPERTS * S // tm + E
    tile_ids = jnp.arange(n_tiles, dtype=jnp.int32)
    tile_expert = jnp.minimum(jnp.sum(tile_ids[:, None] >= tile_end[None, :], axis=1), E - 1).astype(jnp.int32)
    tile_valid = (tile_ids < tile_end[E - 1]).astype(jnp.int32)
    e0 = meta[:, 0].astype(jnp.int32)
    e1 = meta[:, 1].astype(jnp.int32)
    d0 = start[e0] + meta[:, 4].astype(jnp.int32)
    d1 = start[e1] + meta[:, 5].astype(jnp.int32)
    ts = _tile(S, 256)
    xperm = _dispatch(d0, d1, hf, n_tiles * tm, ts)
    y = _ffn_grouped(tile_expert, tile_valid, xperm, wg, wu, wd, tm, _ffn_chunk(wg.shape[2], GROUPED_FF_CHUNK))
    return _combine_ln(d0, d1, meta, hf, y, ln_g, ln_b, alpha, ts, want_bf16)


def kernel(x, w_in, conv_dw_w, conv_dw_b, conv_ln_g, conv_ln_b, w_out, ln_mix_g, ln_mix_b, ln_ffn_g, ln_ffn_b,
           ffn_w_gate, ffn_w_up, ffn_w_down, router_w, expert_w_gate, expert_w_up, expert_w_down):
    batch, S, D = x.shape
    assert batch == 1
    depth = w_in.shape[0]
    conv_w = conv_dw_w.shape[2]
    attn_w = (w_in.shape[2] - 2 * conv_w) // 3
    n_heads = attn_w // HEAD_DIM
    alpha = (2.0 * depth) ** 0.25

    half = HEAD_DIM // 2
    inv_freq = jnp.power(ROPE_THETA, -jnp.arange(half, dtype=F32) / half)
    ang = jnp.arange(S, dtype=jnp.int32).astype(F32)[:, None] * inv_freq[None, :]
    cos = jnp.concatenate([jnp.cos(ang), jnp.cos(ang)], axis=-1)
    sin = jnp.concatenate([-jnp.sin(ang), jnp.sin(ang)], axis=-1)
    scale = HEAD_DIM ** -0.5 * LOG2_E
    cos_tab = jnp.stack([cos * scale, cos])
    sin_tab = jnp.stack([sin * scale, sin])

    hf = x[0]
    hb = hf.astype(BF16)
    tm_proj = _tile(S, 1024)
    for l in range(depth):
        w_in_b = w_in[l].astype(BF16)
        qkv = _qkv_proj(hb, w_in_b, cos_tab, sin_tab, attn_w, tm_proj, _tile(attn_w, 1024))
        u = _glu_proj(hb, w_in_b, attn_w, conv_w, tm_proj, _tile(conv_w, 1024))
        u = _conv_module(u, conv_dw_w[l], conv_dw_b[l], conv_ln_g[l], conv_ln_b[l], _tile(S, 512))
        v_t = qkv[:, 2 * attn_w:].reshape(S // MOBA_BLOCK, MOBA_BLOCK, attn_w).transpose(0, 2, 1)
        attn = _moba_attention(qkv, v_t, n_heads, min(n_heads, ATTN_HEAD_GROUP))
        hf, hb = _out_proj_ln(attn, u, w_out[l].astype(BF16), hf, ln_mix_g[l], ln_mix_b[l], alpha, _tile(S, 512))
        j = l // 2
        if l % 2 == 0:
            tm, tf = _tile(S, 512), _tile(ffn_w_gate.shape[2], 512)
            side = (expert_w_gate[j], expert_w_up[j], expert_w_down[j]) if l + 1 < depth else ()
            if any(_side_cast_spec(w.shape, S // tm, ffn_w_gate.shape[2] // tf) is None for w in side):
                side = ()
            hf, hb, experts_b = _ffn_dense_ln(hb, ffn_w_gate[j].astype(BF16), ffn_w_up[j].astype(BF16),
                                              ffn_w_down[j].astype(BF16), hf, ln_ffn_g[l], ln_ffn_b[l], alpha,
                                              tm, tf, side)
        else:
            if not experts_b:
                experts_b = (expert_w_gate[j].astype(BF16), expert_w_up[j].astype(BF16), expert_w_down[j].astype(BF16))
            hf, hb = _moe_layer(hf, router_w[j], *experts_b, ln_ffn_g[l], ln_ffn_b[l], alpha, l + 1 < depth)
            experts_b = ()
    return hf[None]
```

```python
import functools

import jax
import jax.numpy as jnp
from jax import lax
from jax.experimental import pallas as pl
from jax.experimental.pallas import tpu as pltpu

F32 = jnp.float32
BF16 = jnp.bfloat16

HEAD_DIM = 128
MOBA_BLOCK = 256
MOBA_TOP_K = 3
CONV_KERNEL = 31
ROPE_THETA = 10000.0
LN_EPS = 1e-5
TOP_K_EXPERTS = 2

LANES = 128
SUBLANES = 8
CONV_HALO = 32
CONV_ROWS = 128
MASK_PENALTY = -1e30
LOG2_E = 1.4426950408889634
ATTN_HEAD_GROUP = 4
GROUPED_FF_CHUNK = 512
ROW_DMA_UNROLL = 8
ATTN_UNROLL = 4
BF16_ROWS = 16
ATTN_SUM_ROWS = BF16_ROWS
VMEM_LIMIT = 56 * 1024 * 1024

_DN_T = (((1,), (1,)), ((), ()))


def _params(n_axes):
    return pltpu.CompilerParams(dimension_semantics=("arbitrary",) * n_axes,
                                vmem_limit_bytes=VMEM_LIMIT)


def _sigmoid(x):
    return 1.0 / (1.0 + jnp.exp(-x))


def _layer_norm(z, g, b):
    mu = jnp.mean(z, axis=-1, keepdims=True)
    zc = z - mu
    var = jnp.mean(zc * zc, axis=-1, keepdims=True)
    return zc * lax.rsqrt(var + LN_EPS) * g + b


def _split_bf16(x):
    hi = x.astype(BF16)
    lo = (x - hi.astype(F32)).astype(BF16)
    return hi, lo


def _qkv_kernel(x_ref, w_ref, cos_ref, sin_ref, o_ref, *, n_rot_blocks):
    j = pl.program_id(1)
    acc = jnp.dot(x_ref[...], w_ref[...], preferred_element_type=F32)

    @pl.when(j < n_rot_blocks)
    def _():
        cos = cos_ref[0]
        sin = sin_ref[0]
        for c in range(0, acc.shape[1], HEAD_DIM):
            a = acc[:, c:c + HEAD_DIM]
            r = a * cos + pltpu.roll(a, HEAD_DIM // 2, 1) * sin
            o_ref[:, c:c + HEAD_DIM] = r.astype(o_ref.dtype)

    @pl.when(j >= n_rot_blocks)
    def _():
        o_ref[...] = acc.astype(o_ref.dtype)


def _qkv_proj(xb, w, cos_tab, sin_tab, attn_w, tm, tn):
    S, D = xb.shape
    n_sec = attn_w // tn
    return pl.pallas_call(
        functools.partial(_qkv_kernel, n_rot_blocks=2 * n_sec),
        grid=(S // tm, 3 * n_sec),
        in_specs=[
            pl.BlockSpec((tm, D), lambda i, j: (i, 0)),
            pl.BlockSpec((D, tn), lambda i, j: (0, j)),
            pl.BlockSpec((1, tm, HEAD_DIM), lambda i, j: (jnp.minimum(j // n_sec, 1), i, 0)),
            pl.BlockSpec((1, tm, HEAD_DIM), lambda i, j: (jnp.minimum(j // n_sec, 1), i, 0)),
        ],
        out_specs=pl.BlockSpec((tm, tn), lambda i, j: (i, j)),
        out_shape=jax.ShapeDtypeStruct((S, 3 * attn_w), BF16),
        compiler_params=_params(2),
        name="qkv_proj",
    )(xb, w, cos_tab, sin_tab)


def _glu_kernel(x_ref, wa_ref, wg_ref, o_ref):
    x = x_ref[...]
    a = jnp.dot(x, wa_ref[...], preferred_element_type=F32)
    g = jnp.dot(x, wg_ref[...], preferred_element_type=F32)
    o_ref[...] = a * _sigmoid(g)


def _glu_proj(xb, w, attn_w, conv_w, tm, tn):
    S, D = xb.shape
    a0 = 3 * attn_w // tn
    g0 = (3 * attn_w + conv_w) // tn
    return pl.pallas_call(
        _glu_kernel,
        grid=(S // tm, conv_w // tn),
        in_specs=[
            pl.BlockSpec((tm, D), lambda i, j: (i, 0)),
            pl.BlockSpec((D, tn), lambda i, j: (0, a0 + j)),
            pl.BlockSpec((D, tn), lambda i, j: (0, g0 + j)),
        ],
        out_specs=pl.BlockSpec((tm, tn), lambda i, j: (i, j)),
        out_shape=jax.ShapeDtypeStruct((S, conv_w), F32),
        compiler_params=_params(2),
        name="glu_proj",
    )(xb, w, w)


def _conv_kernel(u_ref, halo_ref, w_ref, b_ref, g_ref, beta_ref, o_ref, ext_ref, y_ref):
    i = pl.program_id(0)
    tc = u_ref.shape[0]
    ext_ref[0:CONV_HALO, :] = jnp.where(i > 0, halo_ref[...], 0.0)
    ext_ref[CONV_HALO:CONV_HALO + tc, :] = u_ref[...]
    bias = b_ref[...]
    g = g_ref[...]
    beta = beta_ref[...]
    first = CONV_HALO - (CONV_KERNEL - 1)
    n_acc = CONV_ROWS // SUBLANES
    sub = lax.broadcasted_iota(jnp.int32, (SUBLANES, LANES), 0)

    def body(r, carry):
        base = pl.multiple_of(r * CONV_ROWS, CONV_ROWS)
        for c in range(0, u_ref.shape[1], LANES):
            win = ext_ref[pl.ds(base, CONV_ROWS + CONV_HALO), c:c + LANES]
            groups = [win[v:v + SUBLANES] for v in range(0, CONV_ROWS + CONV_HALO, SUBLANES)]
            accs = [jnp.zeros((SUBLANES, LANES), F32) + bias[:, c:c + LANES]] * n_acc
            for shift in range(SUBLANES):
                taps = [k for k in range(CONV_KERNEL) if (first + k) % SUBLANES == shift]
                if shift == 0:
                    shifted = groups
                else:
                    shifted = [pltpu.roll(jnp.where(sub >= shift, groups[v], groups[v + 1]), SUBLANES - shift, 0)
                               for v in range(len(groups) - 1)]
                for k in taps:
                    a = (first + k) // SUBLANES
                    wk = w_ref[k:k + 1, c:c + LANES]
                    accs = [accs[v] + wk * shifted[a + v] for v in range(n_acc)]
            y_ref[pl.ds(base, CONV_ROWS), c:c + LANES] = jnp.concatenate(accs, axis=0)
        y = _layer_norm(y_ref[pl.ds(base, CONV_ROWS), :], g, beta)
        o_ref[pl.ds(base, CONV_ROWS), :] = (y * _sigmoid(y)).astype(o_ref.dtype)
        return carry

    lax.fori_loop(0, tc // CONV_ROWS, body, 0)


def _conv_module(u, w, b, g, beta, tc):
    S, C = u.shape
    per = tc // CONV_HALO
    return pl.pallas_call(
        _conv_kernel,
        grid=(S // tc,),
        in_specs=[
            pl.BlockSpec((tc, C), lambda i: (i, 0)),
            pl.BlockSpec((CONV_HALO, C), lambda i: (jnp.maximum(i * per - 1, 0), 0)),
            pl.BlockSpec((CONV_KERNEL, C), lambda i: (0, 0)),
            pl.BlockSpec((1, C), lambda i: (0, 0)),
            pl.BlockSpec((1, C), lambda i: (0, 0)),
            pl.BlockSpec((1, C), lambda i: (0, 0)),
        ],
        out_specs=pl.BlockSpec((tc, C), lambda i: (i, 0)),
        out_shape=jax.ShapeDtypeStruct((S, C), BF16),
        scratch_shapes=[pltpu.VMEM((CONV_HALO + tc, C), F32), pltpu.VMEM((tc, C), F32)],
        compiler_params=_params(1),
        name="conv_module",
    )(u, u, w, b.reshape(1, C), g.reshape(1, C), beta.reshape(1, C))


def _attn_kernel(q_ref, k_ref, vt_ref, o_ref, kmean_ref, *scratch, n_group, n_gate):
    qa_refs = scratch[0:n_group]
    sa_refs = scratch[n_group:2 * n_group]
    sb_refs = scratch[2 * n_group:3 * n_group]
    acc_refs = scratch[3 * n_group:4 * n_group]
    i = pl.program_id(1)
    B = MOBA_BLOCK
    nb = k_ref.shape[0] // B
    gs = range(n_group)
    lane = lax.broadcasted_iota(jnp.int32, (B, LANES), 1)
    blk = lax.broadcasted_iota(jnp.int32, (n_gate, B), 0)
    ones_rows = jnp.ones((ATTN_SUM_ROWS, B), BF16)

    @pl.when(i == 0)
    def _():
        kmean_ref[...] = jnp.zeros_like(kmean_ref)

        def fill(j, carry):
            off = pl.multiple_of(j * B, B)
            for g in gs:
                kb = k_ref[pl.ds(off, B), g * HEAD_DIM:(g + 1) * HEAD_DIM]
                kmean_ref[g, pl.ds(j, 1), :] = jnp.sum(kb.astype(F32), axis=0, keepdims=True) * (1.0 / B)
            return carry

        lax.fori_loop(0, nb, fill, 0)

    neg_inf = -jnp.inf
    key_pos = lax.broadcasted_iota(jnp.int32, (B, B), 0)
    qry_pos = lax.broadcasted_iota(jnp.int32, (B, B), 1)
    off_i = pl.multiple_of(i * B, B)

    def cols(g):
        return slice(g * HEAD_DIM, (g + 1) * HEAD_DIM)

    def v_aug(g, j):
        return jnp.concatenate([vt_ref[j, cols(g), :], ones_rows], axis=0)

    def score(g, j, dst_ref):
        off = pl.multiple_of(j * B, B)
        onehot = jnp.where(lane == j, 1.0, 0.0).astype(BF16)
        k_cat = jnp.concatenate([k_ref[pl.ds(off, B), cols(g)], onehot], axis=1)
        dst_ref[...] = jnp.dot(k_cat, qa_refs[g][...], preferred_element_type=F32)

    def consume_all(j, src_refs, ms):
        ss = [src_refs[g][...] for g in gs]
        m_news = [jnp.maximum(ms[g], jnp.max(ss[g], axis=0, keepdims=True)) for g in gs]
        alphas = [jnp.exp2(ms[g] - m_news[g]) for g in gs]
        ps = [jnp.exp2(ss[g] - m_news[g]).astype(BF16) for g in gs]
        pvs = [jnp.dot(v_aug(g, j), ps[g], preferred_element_type=F32) for g in gs]
        for g in gs:
            acc_refs[g][...] = alphas[g] * acc_refs[g][...] + pvs[g]
        return tuple(m_news)

    q_ts = [q_ref[:, cols(g)].astype(F32).T.astype(BF16) for g in gs]
    kms = [_split_bf16(kmean_ref[g, 0:n_gate, :]) for g in gs]
    gates = [jnp.dot(kms[g][0], q_ts[g], preferred_element_type=F32)
             + jnp.dot(kms[g][1], q_ts[g], preferred_element_type=F32) for g in gs]
    gates = [jnp.where(blk < i, gates[g], neg_inf) for g in gs]
    sels = [blk < 0 for g in gs]
    for _ in range(MOBA_TOP_K):
        tops = [jnp.max(gates[g], axis=0, keepdims=True) for g in gs]
        idxs = [jnp.min(jnp.where(gates[g] == tops[g], blk, LANES), axis=0, keepdims=True) for g in gs]
        hits = [blk == idxs[g] for g in gs]
        sels = [sels[g] | (hits[g] & (tops[g] > neg_inf)) for g in gs]
        gates = [jnp.where(hits[g], neg_inf, gates[g]) for g in gs]
    pad = jnp.full((LANES - n_gate, B), MASK_PENALTY, BF16)
    for g in gs:
        penalty = jnp.where(sels[g], 0.0, MASK_PENALTY).astype(BF16)
        parts = [q_ts[g], penalty] + ([pad] if n_gate < LANES else [])
        qa_refs[g][...] = jnp.concatenate(parts, axis=0)

    ss = [jnp.dot(k_ref[pl.ds(off_i, B), cols(g)], q_ts[g], preferred_element_type=F32) for g in gs]
    ss = [jnp.where(key_pos <= qry_pos, ss[g], neg_inf) for g in gs]
    m0s = [jnp.max(ss[g], axis=0, keepdims=True) for g in gs]
    ps = [jnp.exp2(ss[g] - m0s[g]).astype(BF16) for g in gs]
    for g in gs:
        acc_refs[g][...] = jnp.dot(v_aug(g, i), ps[g], preferred_element_type=F32)
    for g in gs:
        score(g, 0, sa_refs[g])

    def body(step, ms):
        j0 = ATTN_UNROLL * step
        bufs = (sa_refs, sb_refs)
        for t in range(ATTN_UNROLL):
            nxt = jnp.minimum(j0 + t + 1, nb - 1)
            for g in gs:
                score(g, nxt, bufs[(t + 1) % 2][g])
            ms = consume_all(jnp.minimum(j0 + t, nb - 1), bufs[t % 2], ms)
        return ms

    lax.fori_loop(0, (i + ATTN_UNROLL - 1) // ATTN_UNROLL, body, tuple(m0s))
    for g in gs:
        acc = acc_refs[g][...]
        out_t = acc[0:HEAD_DIM] / acc[HEAD_DIM:HEAD_DIM + 1]
        o_ref[:, cols(g)] = out_t.T.astype(o_ref.dtype)


def _moba_attention(qkv, v_t, n_heads, n_group):
    S = qkv.shape[0]
    assert S % MOBA_BLOCK == 0 and S // MOBA_BLOCK <= LANES and n_heads % n_group == 0
    B = MOBA_BLOCK
    nb = S // B
    gw = n_group * HEAD_DIM
    n_gate = min(LANES, -(-nb // BF16_ROWS) * BF16_ROWS)
    resident = pl.Buffered(1)
    scratch = ([pltpu.VMEM((HEAD_DIM + LANES, B), BF16)] * n_group
               + [pltpu.VMEM((B, B), F32)] * (2 * n_group)
               + [pltpu.VMEM((HEAD_DIM + ATTN_SUM_ROWS, B), F32)] * n_group)
    return pl.pallas_call(
        functools.partial(_attn_kernel, n_group=n_group, n_gate=n_gate),
        grid=(n_heads // n_group, nb),
        in_specs=[
            pl.BlockSpec((B, gw), lambda h, i: (i, h)),
            pl.BlockSpec((S, gw), lambda h, i: (0, n_heads // n_group + h), pipeline_mode=resident),
            pl.BlockSpec((nb, gw, B), lambda h, i: (0, h, 0), pipeline_mode=resident),
        ],
        out_specs=pl.BlockSpec((B, gw), lambda h, i: (i, h)),
        out_shape=jax.ShapeDtypeStruct((S, n_heads * HEAD_DIM), BF16),
        scratch_shapes=[pltpu.VMEM((n_group, LANES, HEAD_DIM), F32)] + scratch,
        compiler_params=_params(2),
        name="moba_attention",
    )(qkv, qkv, v_t)


def _outproj_kernel(a_ref, u_ref, w_ref, h_ref, g_ref, b_ref, of_ref, ob_ref, *, alpha):
    da = a_ref.shape[1]
    mix = (jnp.dot(a_ref[...], w_ref[0:da, :], preferred_element_type=F32)
           + jnp.dot(u_ref[...], w_ref[da:, :], preferred_element_type=F32))
    y = _layer_norm(alpha * h_ref[...] + mix, g_ref[...], b_ref[...])
    of_ref[...] = y
    ob_ref[...] = y.astype(BF16)


def _out_proj_ln(attn, u, w, h, g, b, alpha, tm):
    S, D = h.shape
    da, du = attn.shape[1], u.shape[1]
    return pl.pallas_call(
        functools.partial(_outproj_kernel, alpha=alpha),
        grid=(S // tm,),
        in_specs=[
            pl.BlockSpec((tm, da), lambda i: (i, 0)),
            pl.BlockSpec((tm, du), lambda i: (i, 0)),
            pl.BlockSpec((da + du, D), lambda i: (0, 0)),
            pl.BlockSpec((tm, D), lambda i: (i, 0)),
            pl.BlockSpec((1, D), lambda i: (0, 0)),
            pl.BlockSpec((1, D), lambda i: (0, 0)),
        ],
        out_specs=[pl.BlockSpec((tm, D), lambda i: (i, 0)), pl.BlockSpec((tm, D), lambda i: (i, 0))],
        out_shape=[jax.ShapeDtypeStruct((S, D), F32), jax.ShapeDtypeStruct((S, D), BF16)],
        compiler_params=_params(1),
        name="out_proj_ln",
    )(attn, u, w, h, g.reshape(1, D), b.reshape(1, D))


def _swiglu_step(x, wg_ref, wu_ref, wd_ref, acc_ref):
    gg = jnp.dot(x, wg_ref[...], preferred_element_type=F32)
    uu = jnp.dot(x, wu_ref[...], preferred_element_type=F32)
    a = (gg * _sigmoid(gg) * uu).astype(BF16)
    acc_ref[...] += jnp.dot(a, wd_ref[...], preferred_element_type=F32)


def _ffn_dense_kernel(x_ref, wg_ref, wu_ref, wd_ref, h_ref, g_ref, b_ref, *rest, alpha, n_cast):
    cast_in = rest[:n_cast]
    of_ref, ob_ref = rest[n_cast:n_cast + 2]
    cast_out = rest[n_cast + 2:2 * n_cast + 2]
    acc_ref = rest[2 * n_cast + 2]
    f = pl.program_id(1)

    @pl.when(f == 0)
    def _():
        acc_ref[...] = jnp.zeros_like(acc_ref)

    _swiglu_step(x_ref[...], wg_ref, wu_ref, wd_ref, acc_ref)

    for src, dst in zip(cast_in, cast_out):
        dst[...] = src[...].astype(dst.dtype)

    @pl.when(f == pl.num_programs(1) - 1)
    def _():
        y = _layer_norm(alpha * h_ref[...] + acc_ref[...], g_ref[...], b_ref[...])
        of_ref[...] = y
        ob_ref[...] = y.astype(BF16)


def _side_cast_spec(shape, ni, nf):
    E, rows, ncol = shape
    if ncol % (nf * LANES) == 0 and rows % (ni * BF16_ROWS) == 0:
        return pl.BlockSpec((E, rows // ni, ncol // nf), lambda i, f: (0, i, f))
    if rows % (ni * nf * BF16_ROWS) == 0:
        return pl.BlockSpec((E, rows // (ni * nf), ncol), lambda i, f: (0, i * nf + f, 0))
    return None


def _ffn_dense_ln(xb, wg, wu, wd, h, g, b, alpha, tm, tf, expert_w=()):
    S, D = h.shape
    dff = wg.shape[1]
    ni, nf = S // tm, dff // tf
    cast_specs = [_side_cast_spec(w.shape, ni, nf) for w in expert_w]
    n_cast = len(expert_w)
    outs = pl.pallas_call(
        functools.partial(_ffn_dense_kernel, alpha=alpha, n_cast=n_cast),
        grid=(ni, nf),
        in_specs=[
            pl.BlockSpec((tm, D), lambda i, f: (i, 0)),
            pl.BlockSpec((D, tf), lambda i, f: (0, f)),
            pl.BlockSpec((D, tf), lambda i, f: (0, f)),
            pl.BlockSpec((tf, D), lambda i, f: (f, 0)),
            pl.BlockSpec((tm, D), lambda i, f: (i, 0)),
            pl.BlockSpec((1, D), lambda i, f: (0, 0)),
            pl.BlockSpec((1, D), lambda i, f: (0, 0)),
        ] + cast_specs,
        out_specs=[pl.BlockSpec((tm, D), lambda i, f: (i, 0)), pl.BlockSpec((tm, D), lambda i, f: (i, 0))] + cast_specs,
        out_shape=[jax.ShapeDtypeStruct((S, D), F32), jax.ShapeDtypeStruct((S, D), BF16)]
        + [jax.ShapeDtypeStruct(w.shape, BF16) for w in expert_w],
        scratch_shapes=[pltpu.VMEM((tm, D), F32)],
        compiler_params=_params(2),
        name="ffn_dense_ln",
    )(xb, wg, wu, wd, h, g.reshape(1, D), b.reshape(1, D), *expert_w)
    return outs[0], outs[1], tuple(outs[2:])


def _ffn_grouped_kernel(te_ref, tv_ref, x_ref, wg_ref, wu_ref, wd_ref, y_ref, xb_ref, acc_ref):
    t = pl.program_id(0)
    f = pl.program_id(1)
    valid = tv_ref[t] == 1

    @pl.when(valid & (f == 0))
    def _():
        acc_ref[...] = jnp.zeros_like(acc_ref)
        xb_ref[...] = x_ref[...].astype(BF16)

    @pl.when(valid)
    def _():
        _swiglu_step(xb_ref[...], wg_ref, wu_ref, wd_ref, acc_ref)

    @pl.when(f == pl.num_programs(1) - 1)
    def _():
        y_ref[...] = jnp.where(valid, acc_ref[...], 0.0)


def _ffn_grouped(tile_expert, tile_valid, xperm, wg, wu, wd, tm, tf):
    R, D = xperm.shape
    dff = wg.shape[2]
    nf = dff // tf

    def chunk(t, f, tv):
        return jnp.where(tv[t] == 1, f, nf - 1)

    grid_spec = pltpu.PrefetchScalarGridSpec(
        num_scalar_prefetch=2,
        grid=(R // tm, nf),
        in_specs=[
            pl.BlockSpec((tm, D), lambda t, f, te, tv: (t, 0)),
            pl.BlockSpec((None, D, tf), lambda t, f, te, tv: (te[t], 0, chunk(t, f, tv))),
            pl.BlockSpec((None, D, tf), lambda t, f, te, tv: (te[t], 0, chunk(t, f, tv))),
            pl.BlockSpec((None, tf, D), lambda t, f, te, tv: (te[t], chunk(t, f, tv), 0)),
        ],
        out_specs=pl.BlockSpec((tm, D), lambda t, f, te, tv: (t, 0)),
        scratch_shapes=[pltpu.VMEM((tm, D), BF16), pltpu.VMEM((tm, D), F32)],
    )
    return pl.pallas_call(
        _ffn_grouped_kernel,
        grid_spec=grid_spec,
        out_shape=jax.ShapeDtypeStruct((R, D), F32),
        compiler_params=_params(2),
        name="ffn_grouped",
    )(tile_expert, tile_valid, xperm, wg, wu, wd)


def _router_kernel(h_ref, w_ref, meta_ref, cnt_ref, carry_ref, *, n_experts):
    i = pl.program_id(0)
    tr = h_ref.shape[0]

    @pl.when(i == 0)
    def _():
        carry_ref[...] = jnp.zeros_like(carry_ref)

    hh, hl = _split_bf16(h_ref[...])
    wh, wl = _split_bf16(w_ref[...])
    logits = (jnp.dot(hh, wh, preferred_element_type=F32) + jnp.dot(hh, wl, preferred_element_type=F32)
              + jnp.dot(hl, wh, preferred_element_type=F32))
    lane = lax.broadcasted_iota(jnp.int32, (tr, LANES), 1)
    neg_inf = -jnp.inf
    logits = jnp.where(lane < n_experts, logits, neg_inf)
    m0 = jnp.max(logits, axis=1, keepdims=True)
    i0 = jnp.min(jnp.where(logits == m0, lane, LANES), axis=1, keepdims=True)
    hit0 = lane == i0
    rest = jnp.where(hit0, neg_inf, logits)
    m1 = jnp.max(rest, axis=1, keepdims=True)
    i1 = jnp.min(jnp.where(rest == m1, lane, LANES), axis=1, keepdims=True)
    hit1 = lane == i1
    e = jnp.exp(m1 - m0)
    g0 = 1.0 / (1.0 + e)
    g1 = e / (1.0 + e)

    sel = jnp.where(hit0 | hit1, 1.0, 0.0)
    row = lax.broadcasted_iota(jnp.int32, (tr, tr), 0)
    col = lax.broadcasted_iota(jnp.int32, (tr, tr), 1)
    earlier = jnp.where(col < row, 1.0, 0.0).astype(BF16)
    before = jnp.dot(earlier, sel.astype(BF16), preferred_element_type=F32) + carry_ref[...]
    r0 = jnp.sum(jnp.where(hit0, before, 0.0), axis=1, keepdims=True)
    r1 = jnp.sum(jnp.where(hit1, before, 0.0), axis=1, keepdims=True)
    carry_ref[...] += jnp.sum(sel, axis=0, keepdims=True)

    cols = (i0.astype(F32), i1.astype(F32), g0, g1, r0, r1)
    meta = jnp.zeros((tr, LANES), F32)
    for c, val in enumerate(cols):
        meta = jnp.where(lane == c, val, meta)
    meta_ref[...] = meta
    cnt_ref[...] = jnp.broadcast_to(carry_ref[...], cnt_ref.shape)


def _router(h, w_pad, n_experts, tr):
    S, D = h.shape
    return pl.pallas_call(
        functools.partial(_router_kernel, n_experts=n_experts),
        grid=(S // tr,),
        in_specs=[pl.BlockSpec((tr, D), lambda i: (i, 0)), pl.BlockSpec((D, LANES), lambda i: (0, 0))],
        out_specs=[pl.BlockSpec((tr, LANES), lambda i: (i, 0)), pl.BlockSpec((8, LANES), lambda i: (0, 0))],
        out_shape=[jax.ShapeDtypeStruct((S, LANES), F32), jax.ShapeDtypeStruct((8, LANES), F32)],
        scratch_shapes=[pltpu.VMEM((1, LANES), F32)],
        compiler_params=_params(1),
        name="router",
    )(h, w_pad)


def _row_copy(src_ref, src_row, dst_ref, dst_row, sem):
    return pltpu.make_async_copy(src_ref.at[pl.ds(src_row, 1), :], dst_ref.at[pl.ds(dst_row, 1), :], sem)


def _start_row(copy, which):
    copy.start(priority=which)


def _wait_row(copy, which):
    del which
    copy.wait()


def _dispatch_kernel(d0_ref, d1_ref, h_ref, xin_ref, xout_ref, buf_ref, sem):
    del xin_ref
    i = pl.program_id(0)
    n = pl.num_programs(0)
    ts = h_ref.shape[0]
    slot = i % 2

    def rows(step, s, act):
        base = step * ts

        def body(r, carry):
            act(_row_copy(buf_ref.at[s], r, xout_ref, d0_ref[base + r], sem.at[s]), 0)
            act(_row_copy(buf_ref.at[s], r, xout_ref, d1_ref[base + r], sem.at[s]), 1)
            return carry

        lax.fori_loop(0, ts, body, 0, unroll=ROW_DMA_UNROLL)

    @pl.when(i >= 2)
    def _():
        rows(i - 2, slot, _wait_row)

    buf_ref[slot] = h_ref[...]
    rows(i, slot, _start_row)

    @pl.when(i == n - 1)
    def _():
        @pl.when(i >= 1)
        def _():
            rows(i - 1, 1 - slot, _wait_row)

        rows(i, slot, _wait_row)


def _dispatch(d0, d1, h, n_rows, ts):
    S, D = h.shape
    grid_spec = pltpu.PrefetchScalarGridSpec(
        num_scalar_prefetch=2,
        grid=(S // ts,),
        in_specs=[pl.BlockSpec((ts, D), lambda i, a, b: (i, 0)), pl.BlockSpec(memory_space=pl.ANY)],
        out_specs=pl.BlockSpec(memory_space=pl.ANY),
        scratch_shapes=[pltpu.VMEM((2, ts, D), F32), pltpu.SemaphoreType.DMA((2,))],
    )
    return pl.pallas_call(
        _dispatch_kernel,
        grid_spec=grid_spec,
        out_shape=jax.ShapeDtypeStruct((n_rows, D), F32),
        input_output_aliases={3: 0},
        compiler_params=_params(1),
        name="moe_dispatch",
    )(d0, d1, h, jnp.zeros((n_rows, D), F32))


def _combine_kernel(d0_ref, d1_ref, meta_ref, h_ref, y_ref, g_ref, b_ref, *rest, alpha):
    out_refs = rest[:-3]
    y0_ref, y1_ref, sem = rest[-3:]
    i = pl.program_id(0)
    n = pl.num_programs(0)
    ts = h_ref.shape[0]
    slot = i % 2

    def rows(step, s, act):
        base = step * ts

        def body(r, carry):
            act(_row_copy(y_ref, d0_ref[base + r], y0_ref.at[s], r, sem.at[s]), 0)
            act(_row_copy(y_ref, d1_ref[base + r], y1_ref.at[s], r, sem.at[s]), 1)
            return carry

        lax.fori_loop(0, ts, body, 0, unroll=ROW_DMA_UNROLL)

    @pl.when(i == 0)
    def _():
        rows(0, 0, _start_row)

    @pl.when(i + 1 < n)
    def _():
        rows(i + 1, 1 - slot, _start_row)

    rows(i, slot, _wait_row)
    meta = meta_ref[...]
    g0 = meta[:, 2:3]
    g1 = meta[:, 3:4]
    z = alpha * h_ref[...] + g0 * y0_ref[slot] + g1 * y1_ref[slot]
    y = _layer_norm(z, g_ref[...], b_ref[...])
    for o_ref in out_refs:
        o_ref[...] = y.astype(o_ref.dtype)


def _combine_ln(d0, d1, meta, h, y, g, b, alpha, ts, want_bf16):
    S, D = h.shape
    out_dtypes = [F32, BF16] if want_bf16 else [F32]
    grid_spec = pltpu.PrefetchScalarGridSpec(
        num_scalar_prefetch=2,
        grid=(S // ts,),
        in_specs=[
            pl.BlockSpec((ts, LANES), lambda i, a, b: (i, 0)),
            pl.BlockSpec((ts, D), lambda i, a, b: (i, 0)),
            pl.BlockSpec(memory_space=pl.ANY),
            pl.BlockSpec((1, D), lambda i, a, b: (0, 0)),
            pl.BlockSpec((1, D), lambda i, a, b: (0, 0)),
        ],
        out_specs=[pl.BlockSpec((ts, D), lambda i, a, b: (i, 0)) for _ in out_dtypes],
        scratch_shapes=[pltpu.VMEM((2, ts, D), F32), pltpu.VMEM((2, ts, D), F32), pltpu.SemaphoreType.DMA((2,))],
    )
    outs = pl.pallas_call(
        functools.partial(_combine_kernel, alpha=alpha),
        grid_spec=grid_spec,
        out_shape=[jax.ShapeDtypeStruct((S, D), dt) for dt in out_dtypes],
        compiler_params=_params(1),
        name="moe_combine_ln",
    )(d0, d1, meta, h, y, g.reshape(1, D), b.reshape(1, D))
    return outs[0], (outs[1] if want_bf16 else None)


def _tile(n, pref):
    t = min(n, pref)
    assert n % t == 0
    return t


def _ffn_chunk(dff, limit):
    return max(t for t in range(LANES, min(dff, limit) + 1, LANES) if dff % t == 0)


def _moe_layer(hf, router_w, wg, wu, wd, ln_g, ln_b, alpha, want_bf16):
    S, D = hf.shape
    E = wg.shape[0]
    tm = _tile(S, 512)
    w_pad = jnp.zeros((D, LANES), F32).at[:, :E].set(router_w)
    meta, cnt = _router(hf, w_pad, E, _tile(S, 256))
    counts = cnt[0, :E].astype(jnp.int32)
    tiles_per = (counts + tm - 1) // tm
    tile_end = jnp.cumsum(tiles_per)
    start = (tile_end - tiles_per) * tm
    n_tiles = TOP_K_EXPERTS * S // tm + E
    tile_ids = jnp.arange(n_tiles, dtype=jnp.int32)
    tile_expert = jnp.minimum(jnp.sum(tile_ids[:, None] >= tile_end[None, :], axis=1), E - 1).astype(jnp.int32)
    tile_valid = (tile_ids < tile_end[E - 1]).astype(jnp.int32)
    e0 = meta[:, 0].astype(jnp.int32)
    e1 = meta[:, 1].astype(jnp.int32)
    d0 = start[e0] + meta[:, 4].astype(jnp.int32)
    d1 = start[e1] + meta[:, 5].astype(jnp.int32)
    ts = _tile(S, 256)
    xperm = _dispatch(d0, d1, hf, n_tiles * tm, ts)
    y = _ffn_grouped(tile_expert, tile_valid, xperm, wg, wu, wd, tm, _ffn_chunk(wg.shape[2], GROUPED_FF_CHUNK))
    return _combine_ln(d0, d1, meta, hf, y, ln_g, ln_b, alpha, ts, want_bf16)


def kernel(x, w_in, conv_dw_w, conv_dw_b, conv_ln_g, conv_ln_b, w_out, ln_mix_g, ln_mix_b, ln_ffn_g, ln_ffn_b,
           ffn_w_gate, ffn_w_up, ffn_w_down, router_w, expert_w_gate, expert_w_up, expert_w_down):
    batch, S, D = x.shape
    assert batch == 1
    depth = w_in.shape[0]
    conv_w = conv_dw_w.shape[2]
    attn_w = (w_in.shape[2] - 2 * conv_w) // 3
    n_heads = attn_w // HEAD_DIM
    alpha = (2.0 * depth) ** 0.25

    half = HEAD_DIM // 2
    inv_freq = jnp.power(ROPE_THETA, -jnp.arange(half, dtype=F32) / half)
    ang = jnp.arange(S, dtype=jnp.int32).astype(F32)[:, None] * inv_freq[None, :]
    cos = jnp.concatenate([jnp.cos(ang), jnp.cos(ang)], axis=-1)
    sin = jnp.concatenate([-jnp.sin(ang), jnp.sin(ang)], axis=-1)
    scale = HEAD_DIM ** -0.5 * LOG2_E
    cos_tab = jnp.stack([cos * scale, cos])
    sin_tab = jnp.stack([sin * scale, sin])

    hf = x[0]
    hb = hf.astype(BF16)
    tm_proj = _tile(S, 1024)
    for l in range(depth):
        w_in_b = w_in[l].astype(BF16)
        qkv = _qkv_proj(hb, w_in_b, cos_tab, sin_tab, attn_w, tm_proj, _tile(attn_w, 1024))
        u = _glu_proj(hb, w_in_b, attn_w, conv_w, tm_proj, _tile(conv_w, 1024))
        u = _conv_module(u, conv_dw_w[l], conv_dw_b[l], conv_ln_g[l], conv_ln_b[l], _tile(S, 512))
        v_t = qkv[:, 2 * attn_w:].reshape(S // MOBA_BLOCK, MOBA_BLOCK, attn_w).transpose(0, 2, 1)
        attn = _moba_attention(qkv, v_t, n_heads, min(n_heads, ATTN_HEAD_GROUP))
        hf, hb = _out_proj_ln(attn, u, w_out[l].astype(BF16), hf, ln_mix_g[l], ln_mix_b[l], alpha, _tile(S, 512))
        j = l // 2
        if l % 2 == 0:
            tm, tf = _tile(S, 512), _tile(ffn_w_gate.shape[2], 512)
            side = (expert_w_gate[j], expert_w_up[j], expert_w_down[j]) if l + 1 < depth else ()
            if any(_side_cast_spec(w.shape, S // tm, ffn_w_gate.shape[2] // tf) is None for w in side):
                side = ()
            hf, hb, experts_b = _ffn_dense_ln(hb, ffn_w_gate[j].astype(BF16), ffn_w_up[j].astype(BF16),
                                              ffn_w_down[j].astype(BF16), hf, ln_ffn_g[l], ln_ffn_b[l], alpha,
                                              tm, tf, side)
        else:
            if not experts_b:
                experts_b = (expert_w_gate[j].astype(BF16), expert_w_up[j].astype(BF16), expert_w_down[j].astype(BF16))
            hf, hb = _moe_layer(hf, router_w[j], *experts_b, ln_ffn_g[l], ln_ffn_b[l], alpha, l + 1 < depth)
            experts_b = ()
    return hf[None]
```
